```python
import jax, jax.numpy as jnp
from jax import lax
import numpy as np

D_MODEL = 1024
BATCH = 8
SEQ = 4096
DEPTH = 4

GRID_W = 64
CTX_LEN = 256
EPS = 1e-6

W_CONV = 1024
CONV_K = 3
N_HEADS_NA = 16
HEAD_DIM_NA = 64
W_NA = N_HEADS_NA * HEAD_DIM_NA
WIN_R = 8
WIN_C = 16
OFF_AB = 0
OFF_AC = W_CONV
OFF_AX = 2 * W_CONV
OFF_AG = 3 * W_CONV
OFF_Q = 4 * W_CONV
OFF_K = OFF_Q + W_NA
OFF_V = OFF_K + W_NA
OFF_BG = OFF_V + W_NA
E_IN = OFF_BG + W_NA
E_MIX = W_CONV + W_NA

W_RNN = 2048
N_BLK = 8
BLK = W_RNN // N_BLK
CONV_K_RNN = 4
LRU_C = 8.0
N_DIR = 2

N_EVEN = (DEPTH + 1) // 2
N_ODD = DEPTH // 2

kernel_name = "hybrid_conv_natten_rglru_diffusion_block"


def _rmsnorm(x, g):
    xf = x.astype(jnp.float32)
    y = xf * lax.rsqrt(jnp.mean(xf * xf, axis=-1, keepdims=True) + EPS)
    return (y * g.astype(jnp.float32)).astype(x.dtype)


def _dwconv(x, w, left):
    k = w.shape[0]
    return lax.conv_general_dilated(
        x, w[:, None, :].astype(x.dtype), window_strides=(1,),
        padding=[(left, k - 1 - left)], dimension_numbers=("NWC", "WIO", "NWC"),
        feature_group_count=x.shape[-1])


def _context_attention(q, k, v):
    bsz, L, H, dh = q.shape
    s = jnp.einsum("bqhd,bkhd->bhqk", q, k).astype(jnp.float32)
    p = jax.nn.softmax(s, axis=-1).astype(q.dtype)
    return jnp.einsum("bhqk,bkhd->bqhd", p, v).reshape(bsz, L, H * dh)


def _neighbourhood_attention(q, k, v, k_ctx, v_ctx, rpb):
    bsz, S, H, dh = q.shape
    rows = S // GRID_W
    kr = min(WIN_R, rows)
    n_win = kr * WIN_C
    qg = q.reshape(bsz, rows, GRID_W, H, dh)
    kg = k.reshape(bsz, rows, GRID_W, H, dh)
    vg = v.reshape(bsz, rows, GRID_W, H, dh)
    cols = np.arange(GRID_W)
    col_start = np.clip(cols - WIN_C // 2, 0, GRID_W - WIN_C)
    col_idx = col_start[:, None] + np.arange(WIN_C)[None, :]
    col_bias_idx = col_idx - cols[:, None] + (WIN_C - 1)

    def row_block(r):
        rs = jnp.clip(r - WIN_R // 2, 0, rows - kr)
        k_band = lax.dynamic_slice_in_dim(kg, rs, kr, axis=1)
        v_band = lax.dynamic_slice_in_dim(vg, rs, kr, axis=1)
        k_win = k_band[:, :, col_idx]
        v_win = v_band[:, :, col_idx]
        q_r = lax.dynamic_index_in_dim(qg, r, axis=1, keepdims=False)
        row_bias_idx = rs + jnp.arange(kr) - r + (WIN_R - 1)
        bias = rpb[:, row_bias_idx[None, :, None], col_bias_idx[:, None, :]]
        s_win = jnp.einsum("bqhd,brqjhd->bhqrj", q_r, k_win).astype(jnp.float32) + bias.astype(jnp.float32)
        s_ctx = jnp.einsum("bqhd,bkhd->bhqk", q_r, k_ctx).astype(jnp.float32)
        s = jnp.concatenate([s_win.reshape(bsz, H, GRID_W, n_win), s_ctx], axis=-1)
        p = jax.nn.softmax(s, axis=-1).astype(q.dtype)
        p_win = p[..., :n_win].reshape(bsz, H, GRID_W, kr, WIN_C)
        return (jnp.einsum("bhqrj,brqjhd->bqhd", p_win, v_win)
                + jnp.einsum("bhqk,bkhd->bqhd", p[..., n_win:], v_ctx))

    out = lax.map(row_block, jnp.arange(rows))
    return jnp.moveaxis(out, 0, 1).reshape(bsz, S, H * dh)


def _even_mixer(u_lat, u_ctx, w_in, w_conv, rpb, w_out, need_ctx):
    bsz, S, _ = u_lat.shape
    Lc = u_ctx.shape[1]
    scale = HEAD_DIM_NA ** -0.5
    z = u_lat @ w_in
    y_a = z[..., OFF_AB:OFF_AC] * _dwconv(z[..., OFF_AC:OFF_AX] * z[..., OFF_AX:OFF_AG], w_conv, CONV_K // 2)
    y_a = y_a * jax.nn.silu(z[..., OFF_AG:OFF_Q])
    base = 0 if need_ctx else OFF_K
    zc = u_ctx @ (w_in if need_ctx else w_in[:, OFF_K:OFF_BG])
    k_c = zc[..., OFF_K - base:OFF_V - base].reshape(bsz, Lc, N_HEADS_NA, HEAD_DIM_NA)
    v_c = zc[..., OFF_V - base:OFF_BG - base].reshape(bsz, Lc, N_HEADS_NA, HEAD_DIM_NA)
    q = z[..., OFF_Q:OFF_K].reshape(bsz, S, N_HEADS_NA, HEAD_DIM_NA) * scale
    k = z[..., OFF_K:OFF_V].reshape(bsz, S, N_HEADS_NA, HEAD_DIM_NA)
    v = z[..., OFF_V:OFF_BG].reshape(bsz, S, N_HEADS_NA, HEAD_DIM_NA)
    y_b = _neighbourhood_attention(q, k, v, k_c, v_c, rpb) * jax.nn.silu(z[..., OFF_BG:E_IN])
    y_lat = jnp.concatenate([y_a, y_b], axis=-1) @ w_out
    if not need_ctx:
        return y_lat, None
    yc_a = zc[..., OFF_AB:OFF_AC] * _dwconv(zc[..., OFF_AC:OFF_AX] * zc[..., OFF_AX:OFF_AG], w_conv, CONV_K // 2)
    yc_a = yc_a * jax.nn.silu(zc[..., OFF_AG:OFF_Q])
    q_c = zc[..., OFF_Q:OFF_K].reshape(bsz, Lc, N_HEADS_NA, HEAD_DIM_NA) * scale
    yc_b = _context_attention(q_c, k_c, v_c) * jax.nn.silu(zc[..., OFF_BG:E_IN])
    y_ctx = jnp.concatenate([yc_a, yc_b], axis=-1) @ w_out
    return y_lat, y_ctx


def _rglru_coeffs(xs, w_ra, b_ra, w_ri, b_ri, lam):
    bsz, L, _ = xs.shape
    xb = xs.reshape(bsz, L, N_BLK, BLK)

    def blockdiag(w, bias):
        return (jnp.einsum("blnk,dnkm->dblnm", xb, w).reshape(N_DIR, bsz, L, W_RNN)
                + bias[:, None, None, :]).astype(jnp.float32)

    r = jax.nn.sigmoid(blockdiag(w_ra, b_ra))
    i = jax.nn.sigmoid(blockdiag(w_ri, b_ri))
    log_a = LRU_C * r * jax.nn.log_sigmoid(lam.astype(jnp.float32))[:, None, None, :]
    a = jnp.exp(log_a)
    b = jnp.sqrt(-jnp.expm1(2.0 * log_a)) * i * xs.astype(jnp.float32)[None]
    return a, b


def _bidir_scan(a, b, h0):
    def scan_order(z):
        return jnp.moveaxis(jnp.stack([z[0], jnp.flip(z[1], axis=1)]), 2, 0)

    def step(h, ab):
        h = ab[0] * h + ab[1]
        return h, h

    h_T, hs = lax.scan(step, h0, (scan_order(a), scan_order(b)))
    y = hs[:, 0] + jnp.flip(hs[:, 1], axis=0)
    return h_T, jnp.moveaxis(y, 0, 1)


def _odd_mixer(u_lat, u_ctx, w_in, conv_w, conv_b, w_ra, b_ra, w_ri, b_ri, lam, w_out, need_ctx):
    bsz = u_lat.shape[0]
    left = (CONV_K_RNN - 1) // 2
    z = u_lat @ w_in
    zc = u_ctx @ (w_in if need_ctx else w_in[:, :W_RNN])
    x_l = _dwconv(z[..., :W_RNN], conv_w, left) + conv_b
    x_c = _dwconv(zc[..., :W_RNN], conv_w, left) + conv_b
    a_c, b_c = _rglru_coeffs(x_c, w_ra, b_ra, w_ri, b_ri, lam)
    h_ctx_T, y_c = _bidir_scan(a_c, b_c, jnp.zeros((N_DIR, bsz, W_RNN), jnp.float32))
    a_l, b_l = _rglru_coeffs(x_l, w_ra, b_ra, w_ri, b_ri, lam)
    _, y_l = _bidir_scan(a_l, b_l, h_ctx_T)
    y_lat = (y_l.astype(u_lat.dtype) * jax.nn.silu(z[..., W_RNN:])) @ w_out
    if not need_ctx:
        return y_lat, None
    y_ctx = (y_c.astype(u_ctx.dtype) * jax.nn.silu(zc[..., W_RNN:])) @ w_out
    return y_lat, y_ctx


def setup_inputs(seed: int = 0) -> dict:
    key = jax.random.key(seed)
    ks = jax.random.split(key, 24)
    f32 = jnp.float32

    def nrm(k, shape, s):
        return jax.random.normal(k, shape, f32) * s

    u = jax.random.uniform(ks[20], (N_ODD, N_DIR, W_RNN), f32, 0.9, 0.999)
    p = u ** (1.0 / LRU_C)
    lru_lam = jnp.log(p) - jnp.log1p(-p)
    return {
        "x": nrm(ks[0], (BATCH, SEQ, D_MODEL), 1.0),
        "c": nrm(ks[1], (BATCH, D_MODEL), 1.0),
        "ctx": nrm(ks[2], (BATCH, CTX_LEN, D_MODEL), 1.0),
        "c_ctx": nrm(ks[3], (D_MODEL,), 1.0),
        "norm_g": 1.0 + nrm(ks[4], (DEPTH, D_MODEL), 0.05),
        "w_ada": nrm(ks[5], (DEPTH, D_MODEL, 3 * D_MODEL), 0.5 * D_MODEL ** -0.5),
        "b_ada": nrm(ks[6], (DEPTH, 3 * D_MODEL), 0.02),
        "w_in_e": nrm(ks[7], (N_EVEN, D_MODEL, E_IN), D_MODEL ** -0.5),
        "conv_a": nrm(ks[8], (N_EVEN, CONV_K, W_CONV), CONV_K ** -0.5),
        "rpb": nrm(ks[9], (N_EVEN, N_HEADS_NA, 2 * WIN_R - 1, 2 * WIN_C - 1), 0.05),
        "w_out_e": nrm(ks[10], (N_EVEN, E_MIX, D_MODEL), E_MIX ** -0.5),
        "w_in_o": nrm(ks[11], (N_ODD, D_MODEL, 2 * W_RNN), D_MODEL ** -0.5),
        "conv_c": nrm(ks[12], (N_ODD, CONV_K_RNN, W_RNN), CONV_K_RNN ** -0.5),
        "conv_c_b": nrm(ks[13], (N_ODD, W_RNN), 0.02),
        "w_ra": nrm(ks[14], (N_ODD, N_DIR, N_BLK, BLK, BLK), BLK ** -0.5),
        "b_ra": nrm(ks[15], (N_ODD, N_DIR, W_RNN), 0.02),
        "w_ri": nrm(ks[16], (N_ODD, N_DIR, N_BLK, BLK, BLK), BLK ** -0.5),
        "b_ri": nrm(ks[17], (N_ODD, N_DIR, W_RNN), 0.02),
        "lru_lam": lru_lam,
        "w_out_o": nrm(ks[18], (N_ODD, W_RNN, D_MODEL), W_RNN ** -0.5),
        "final_g": 1.0 + nrm(ks[19], (D_MODEL,), 0.05),
    }


def reference(x, c, ctx, c_ctx, norm_g, w_ada, b_ada, w_in_e, conv_a, rpb, w_out_e,
              w_in_o, conv_c, conv_c_b, w_ra, b_ra, w_ri, b_ri, lru_lam, w_out_o, final_g):
    D = D_MODEL
    s_c = jax.nn.silu(c)
    s_cc = jax.nn.silu(c_ctx)
    h_lat, h_ctx = x, ctx
    for layer in range(DEPTH):
        need_ctx = layer < DEPTH - 1
        mod = s_c @ w_ada[layer] + b_ada[layer]
        shift, scale, gate = mod[:, None, :D], mod[:, None, D:2 * D], mod[:, None, 2 * D:]
        mod_c = s_cc @ w_ada[layer] + b_ada[layer]
        u_lat = _rmsnorm(h_lat, norm_g[layer]) * (1.0 + scale) + shift
        u_ctx = _rmsnorm(h_ctx, norm_g[layer]) * (1.0 + mod_c[D:2 * D]) + mod_c[:D]
        i = layer // 2
        if layer % 2 == 0:
            y_lat, y_ctx = _even_mixer(u_lat, u_ctx, w_in_e[i], conv_a[i], rpb[i], w_out_e[i], need_ctx)
        else:
            y_lat, y_ctx = _odd_mixer(u_lat, u_ctx, w_in_o[i], conv_c[i], conv_c_b[i], w_ra[i], b_ra[i],
                                      w_ri[i], b_ri[i], lru_lam[i], w_out_o[i], need_ctx)
        h_lat = h_lat + gate * y_lat
        if need_ctx:
            h_ctx = h_ctx + mod_c[2 * D:] * y_ctx
    return _rmsnorm(h_lat, final_g)
```

```python
import functools

import numpy as np
import jax
import jax.numpy as jnp
from jax import lax
from jax.experimental import pallas as pl
from jax.experimental.pallas import tpu as pltpu

GRID_W = 64
EPS = 1e-6
LRU_C = 8.0
MASK_NEG = -1e30
ATTN_ROWS = 4
LANES = 128
VMEM_LIMIT_BYTES = 48 * 1024 * 1024

F32 = jnp.float32
BF16 = jnp.bfloat16


def _cparams(*sem):
    return pltpu.CompilerParams(dimension_semantics=sem, vmem_limit_bytes=VMEM_LIMIT_BYTES)


def _sigmoid(x):
    return 1.0 / (1.0 + jnp.exp(-x))


def _silu(x):
    return x * _sigmoid(x)


def _mod_kernel(s_ref, w_ref, b_ref, o_ref):
    a = _silu(s_ref[...])
    o_ref[...] = jnp.dot(a, w_ref[...], preferred_element_type=F32,
                         precision=lax.Precision.HIGHEST) + b_ref[...]


def _modulation(c, c_ctx, w_ada, b_ada):
    depth, d, n3 = w_ada.shape
    bsz = c.shape[0]
    rows = -(-(bsz + 1) // 8) * 8
    s_in = jnp.zeros((rows, d), F32).at[:bsz].set(c).at[bsz].set(c_ctx)
    tn = d
    return pl.pallas_call(
        _mod_kernel,
        grid=(depth, n3 // tn),
        in_specs=[pl.BlockSpec((rows, d), lambda l, j: (0, 0)),
                  pl.BlockSpec((None, d, tn), lambda l, j: (l, 0, j)),
                  pl.BlockSpec((None, 1, tn), lambda l, j: (l, 0, j))],
        out_specs=pl.BlockSpec((None, rows, tn), lambda l, j: (l, 0, j)),
        out_shape=jax.ShapeDtypeStruct((depth, rows, n3), F32),
        compiler_params=_cparams("parallel", "parallel"),
        name="adaln_mod",
    )(s_in, w_ada, b_ada.reshape(depth, 1, n3))


def _in_proj_kernel(x_ref, g_ref, sc_ref, sh_ref, w_ref, o_ref, u_ref):
    @pl.when(pl.program_id(2) == 0)
    def _():
        x = x_ref[...]
        y = x * lax.rsqrt(jnp.mean(x * x, axis=-1, keepdims=True) + EPS) * g_ref[...]
        u_ref[...] = (y * (1.0 + sc_ref[...]) + sh_ref[...]).astype(u_ref.dtype)

    o_ref[...] = jnp.dot(u_ref[...], w_ref[...], preferred_element_type=F32).astype(o_ref.dtype)


def _in_proj(h, h_layout, bsz, g, scale, shift, w, out_layout):
    d, n = w.shape
    t = h.shape[1] if h_layout == "bsd" else h.shape[0]
    tm = min(t, 1024)
    tn = min(n, 1024)
    nj = n // tn
    if h_layout == "bsd":
        x_spec = pl.BlockSpec((None, tm, d), lambda b, i, j: (b, i, 0))
    else:
        x_spec = pl.BlockSpec((tm, d), lambda b, i, j: (i, b))
    if out_layout == "bsd":
        o_spec = pl.BlockSpec((None, tm, tn), lambda b, i, j: (b, i, j))
        o_shape = jax.ShapeDtypeStruct((bsz, t, n), BF16)
    else:
        o_spec = pl.BlockSpec((tm, tn), lambda b, i, j: (i, b * nj + j))
        o_shape = jax.ShapeDtypeStruct((t, bsz * n), BF16)
    vec_spec = pl.BlockSpec((None, 1, d), lambda b, i, j: (b, 0, 0))
    return pl.pallas_call(
        _in_proj_kernel,
        grid=(bsz, t // tm, nj),
        in_specs=[x_spec,
                  pl.BlockSpec((1, d), lambda b, i, j: (0, 0)),
                  vec_spec, vec_spec,
                  pl.BlockSpec((d, tn), lambda b, i, j: (0, j))],
        out_specs=o_spec,
        out_shape=o_shape,
        scratch_shapes=[pltpu.VMEM((tm, d), BF16)],
        compiler_params=_cparams("parallel", "parallel", "arbitrary"),
        name="norm_in_proj",
    )(h, g.reshape(1, d), scale, shift, w)


def _out_proj_kernel(ya_ref, yb_ref, wa_ref, wb_ref, h_ref, gt_ref, o_ref):
    proj = jnp.dot(ya_ref[...], wa_ref[...], preferred_element_type=F32)
    proj += jnp.dot(yb_ref[...], wb_ref[...], preferred_element_type=F32)
    o_ref[...] = h_ref[...] + gt_ref[...] * proj


def _out_proj(ya, yb, wa, wb, h, h_layout, gate, out_layout):
    bsz, t, wa_in = ya.shape
    wb_in = yb.shape[2]
    d = wa.shape[1]
    tm = min(t, 512)

    def spec(layout):
        if layout == "bsd":
            return pl.BlockSpec((None, tm, d), lambda b, i: (b, i, 0))
        return pl.BlockSpec((tm, d), lambda b, i: (i, b))

    o_shape = (bsz, t, d) if out_layout == "bsd" else (t, bsz * d)
    return pl.pallas_call(
        _out_proj_kernel,
        grid=(bsz, t // tm),
        in_specs=[pl.BlockSpec((None, tm, wa_in), lambda b, i: (b, i, 0)),
                  pl.BlockSpec((None, tm, wb_in), lambda b, i: (b, i, 0)),
                  pl.BlockSpec((wa_in, d), lambda b, i: (0, 0)),
                  pl.BlockSpec((wb_in, d), lambda b, i: (0, 0)),
                  spec(h_layout),
                  pl.BlockSpec((None, 1, d), lambda b, i: (b, 0, 0))],
        out_specs=spec(out_layout),
        out_shape=jax.ShapeDtypeStruct(o_shape, F32),
        compiler_params=_cparams("parallel", "parallel"),
        name="out_proj_residual",
    )(ya, yb, wa, wb, h, gate)


CONV_CHUNK = 256
CONV_PAD = 8


def _conv_mix_kernel(zb_ref, zc_ref, zx_ref, zg_ref, w_ref, o_ref, p_ref, *, t, chunk):
    tc = o_ref.shape[1]
    pad = jnp.zeros((CONV_PAD, tc), F32)
    p_ref[0:CONV_PAD, :] = pad
    p_ref[t + CONV_PAD:t + 2 * CONV_PAD, :] = pad
    for s in range(0, t, chunk):
        p_ref[CONV_PAD + s:CONV_PAD + s + chunk, :] = (
            zc_ref[s:s + chunk, :].astype(F32) * zx_ref[s:s + chunk, :].astype(F32))
    w = w_ref[...]
    n_ext = chunk + 2 * CONV_PAD
    for s in range(0, t, chunk):
        ext = p_ref[s:s + n_ext, :]
        prev = pltpu.roll(ext, 1, axis=0)[CONV_PAD:CONV_PAD + chunk]
        nxt = pltpu.roll(ext, n_ext - 1, axis=0)[CONV_PAD:CONV_PAD + chunk]
        cur = ext[CONV_PAD:CONV_PAD + chunk]
        conv = w[0:1] * prev + w[1:2] * cur + w[2:3] * nxt
        y = zb_ref[s:s + chunk, :].astype(F32) * conv * _silu(zg_ref[s:s + chunk, :].astype(F32))
        o_ref[s:s + chunk, :] = y.astype(o_ref.dtype)


def _conv_mix(z, conv_w, w_conv):
    bsz, t, _ = z.shape
    assert conv_w.shape[0] == 3
    tc = min(w_conv, 256)
    nc = w_conv // tc
    chunk = min(t, CONV_CHUNK)

    def zspec(group):
        return pl.BlockSpec((None, t, tc), lambda b, j, group=group: (b, 0, group * nc + j))

    return pl.pallas_call(
        functools.partial(_conv_mix_kernel, t=t, chunk=chunk),
        grid=(bsz, nc),
        in_specs=[zspec(0), zspec(1), zspec(2), zspec(3),
                  pl.BlockSpec((3, tc), lambda b, j: (0, j))],
        out_specs=pl.BlockSpec((None, t, tc), lambda b, j: (b, 0, j)),
        out_shape=jax.ShapeDtypeStruct((bsz, t, w_conv), BF16),
        scratch_shapes=[pltpu.VMEM((t + 2 * CONV_PAD, tc), F32)],
        compiler_params=_cparams("parallel", "parallel"),
        name="gated_conv3",
    )(z, z, z, z, conv_w)


def _head_masks(dh):
    lane = lax.broadcasted_iota(jnp.int32, (1, LANES), 1)
    return [(lane >= i * dh) & (lane < (i + 1) * dh) for i in range(LANES // dh)]


def _dot_nt(a, b):
    return lax.dot_general(a, b, (((1,), (1,)), ((), ())), preferred_element_type=F32)


def _ctx_attn_kernel(q_ref, k_ref, v_ref, g_ref, o_ref, *, dh, scale):
    masks = _head_masks(dh)
    qs = q_ref[...].astype(F32) * scale
    k = k_ref[...]
    v = v_ref[...]
    out = None
    for msk in masks:
        qh = jnp.where(msk, qs, 0.0).astype(BF16)
        s = _dot_nt(qh, k)
        p = jnp.exp(s - jnp.max(s, axis=-1, keepdims=True))
        o = jnp.dot(p.astype(BF16), v, preferred_element_type=F32) / jnp.sum(p, axis=-1, keepdims=True)
        out = o if out is None else jnp.where(msk, o, out)
    o_ref[...] = (out * _silu(g_ref[...].astype(F32))).astype(o_ref.dtype)


def _ctx_attn(zc, off_q, w_na, dh):
    bsz, lc, _ = zc.shape
    npair = w_na // LANES

    def spec(off):
        return pl.BlockSpec((None, lc, LANES), lambda b, p, off=off: (b, 0, off // LANES + p))

    return pl.pallas_call(
        functools.partial(_ctx_attn_kernel, dh=dh, scale=dh ** -0.5),
        grid=(bsz, npair),
        in_specs=[spec(off_q), spec(off_q + w_na), spec(off_q + 2 * w_na), spec(off_q + 3 * w_na)],
        out_specs=pl.BlockSpec((None, lc, LANES), lambda b, p: (b, 0, p)),
        out_shape=jax.ShapeDtypeStruct((bsz, lc, w_na), BF16),
        compiler_params=_cparams("parallel", "parallel"),
        name="context_attention",
    )(zc, zc, zc, zc)


def _na_attn_kernel(q_ref, k_ref, v_ref, g_ref, kc_ref, vc_ref, bias_ref, o_ref, *,
                    dh, scale, n_groups, rows, win_r):
    masks = _head_masks(dh)
    rq = ATTN_ROWS * GRID_W
    wrows = ATTN_ROWS + win_r
    rk = wrows * GRID_W
    kc = kc_ref[...]
    vc = vc_ref[...]

    def group(g, carry):
        q0 = pl.multiple_of(g * rq, rq)
        ws = jnp.clip(g * ATTN_ROWS - win_r // 2, 0, rows - wrows)
        k0 = pl.multiple_of(ws * GRID_W, GRID_W)
        var = jnp.where(g == 0, 0, jnp.where(g == n_groups - 1, 2, 1))
        qs = q_ref[pl.ds(q0, rq), :].astype(F32) * scale
        kw = k_ref[pl.ds(k0, rk), :]
        vw = v_ref[pl.ds(k0, rk), :]
        out = None
        for hi, msk in enumerate(masks):
            qh = jnp.where(msk, qs, 0.0).astype(BF16)
            s_w = _dot_nt(qh, kw) + bias_ref[hi, var]
            s_c = _dot_nt(qh, kc)
            m = jnp.maximum(jnp.max(s_w, axis=-1, keepdims=True), jnp.max(s_c, axis=-1, keepdims=True))
            p_w = jnp.exp(s_w - m)
            p_c = jnp.exp(s_c - m)
            l = jnp.sum(p_w, axis=-1, keepdims=True) + jnp.sum(p_c, axis=-1, keepdims=True)
            o = (jnp.dot(p_w.astype(BF16), vw, preferred_element_type=F32)
                 + jnp.dot(p_c.astype(BF16), vc, preferred_element_type=F32)) / l
            out = o if out is None else jnp.where(msk, o, out)
        gt = g_ref[pl.ds(q0, rq), :].astype(F32)
        o_ref[pl.ds(q0, rq), :] = (out * _silu(gt)).astype(o_ref.dtype)
        return carry

    lax.fori_loop(0, n_groups, group, 0)


def _na_bias_tables(rpb, rows, win_r, win_c):
    n_heads = rpb.shape[0]
    r_grp = ATTN_ROWS
    wrows = r_grp + win_r
    cols = np.arange(GRID_W)
    cs = np.clip(cols - win_c // 2, 0, GRID_W - win_c)
    col_ok = (cols[None, :] >= cs[:, None]) & (cols[None, :] < cs[:, None] + win_c)
    padded = jnp.pad(rpb.astype(F32), ((0, 0), (0, 0), (GRID_W - win_c, GRID_W - win_c)))
    toep = jnp.stack([padded[:, :, GRID_W - 1 - c:2 * GRID_W - 1 - c] for c in range(GRID_W)], axis=2)
    toep = jnp.where(col_ok[None, None], toep, MASK_NEG)
    masked_tile = jnp.full((n_heads, GRID_W, GRID_W), MASK_NEG, F32)
    tables = []
    for r0 in (0, r_grp, rows - r_grp):
        ws = int(np.clip(r0 - win_r // 2, 0, rows - wrows))
        row_blocks = []
        for i in range(r_grp):
            r = r0 + i
            rs = int(np.clip(r - win_r // 2, 0, rows - win_r))
            tiles = []
            for j in range(wrows):
                rp = ws + j
                tiles.append(toep[:, rp - r + win_r - 1] if rs <= rp < rs + win_r else masked_tile)
            row_blocks.append(jnp.concatenate(tiles, axis=-1))
        tables.append(jnp.concatenate(row_blocks, axis=-2))
    return jnp.stack(tables, axis=1)


def _na_attn(z, zc, off_q, off_kc, w_na, dh, bias):
    bsz, s, _ = z.shape
    lc = zc.shape[1]
    npair = w_na // LANES
    rows = s // GRID_W
    win_r = bias.shape[3] // GRID_W - ATTN_ROWS
    assert rows % ATTN_ROWS == 0 and rows >= ATTN_ROWS + win_r and ATTN_ROWS >= win_r // 2
    assert LANES % dh == 0
    hp = LANES // dh
    n_groups = rows // ATTN_ROWS

    def zspec(off):
        return pl.BlockSpec((None, s, LANES), lambda p, b, off=off: (b, 0, off // LANES + p))

    def cspec(off):
        return pl.BlockSpec((None, lc, LANES), lambda p, b, off=off: (b, 0, off // LANES + p))

    return pl.pallas_call(
        functools.partial(_na_attn_kernel, dh=dh, scale=dh ** -0.5, n_groups=n_groups, rows=rows, win_r=win_r),
        grid=(npair, bsz),
        in_specs=[zspec(off_q), zspec(off_q + w_na), zspec(off_q + 2 * w_na), zspec(off_q + 3 * w_na),
                  cspec(off_kc), cspec(off_kc + w_na),
                  pl.BlockSpec((hp,) + bias.shape[1:], lambda p, b: (p, 0, 0, 0))],
        out_specs=pl.BlockSpec((None, s, LANES), lambda p, b: (b, 0, p)),
        out_shape=jax.ShapeDtypeStruct((bsz, s, w_na), BF16),
        compiler_params=_cparams("parallel", "parallel"),
        name="neighbourhood_attention",
    )(z, z, z, z, zc, zc, bias)


LRU_STEPS = 32


def _lru_coeffs(zx_ref, zp_ref, zn_ref, cw_ref, cb_ref, wg_ref, bg_ref, lam_ref, a_ref, b_ref,
                has_prev, has_next, nb, n_blk, blk):
    n = zx_ref.shape[0]
    k_taps = cw_ref.shape[0]
    left = (k_taps - 1) // 2

    def block(j, carry):
        c0 = pl.multiple_of(j * blk, blk)
        cols = pl.ds(c0, blk)
        x = zx_ref[:, cols].astype(F32)
        xp = jnp.where(has_prev, zp_ref[:, cols].astype(F32), 0.0)[2 * nb - left * nb:]
        xn = jnp.where(has_next, zn_ref[:, cols].astype(F32), 0.0)
        xe = jnp.concatenate([xp, x, xn], axis=0)
        cw = cw_ref[:, cols]
        xl = cb_ref[:, cols]
        for k in range(k_taps):
            xl = xl + cw[k:k + 1] * xe[k * nb:k * nb + n]
        gates = jnp.dot(xl.astype(BF16), wg_ref[j], preferred_element_type=F32) + bg_ref[j]
        r = _sigmoid(gates[:, :blk])
        ig = _sigmoid(gates[:, blk:])
        log_a = (LRU_C * jax.nn.log_sigmoid(lam_ref[:, cols])) * r
        a_ref[:, cols] = jnp.exp(log_a)
        th = jnp.tanh(log_a)
        b_ref[:, cols] = jnp.sqrt(-2.0 * th / (1.0 - th)) * ig * xl
        return carry

    lax.fori_loop(0, n_blk, block, 0)


def _lru_fwd_kernel(zx_ref, zp_ref, zn_ref, cw_ref, cb_ref, wg_ref, bg_ref, lam_ref, h0_ref,
                    hs_ref, hT_ref, a_ref, b_ref, h_ref, *, nb, n_blk, blk, steps):
    i = pl.program_id(0)
    last = pl.num_programs(0) - 1

    @pl.when(i == 0)
    def _():
        h_ref[...] = h0_ref[...]

    _lru_coeffs(zx_ref, zp_ref, zn_ref, cw_ref, cb_ref, wg_ref, bg_ref, lam_ref, a_ref, b_ref,
                i > 0, i < last, nb, n_blk, blk)

    def step(t, h):
        rows = pl.ds(pl.multiple_of(t * nb, nb), nb)
        h = a_ref[rows, :] * h + b_ref[rows, :]
        hs_ref[rows, :] = h
        return h

    h_ref[...] = lax.fori_loop(0, steps, step, h_ref[...])

    @pl.when(i == last)
    def _():
        hT_ref[...] = h_ref[...]


def _lru_bwd_kernel(zx_ref, zp_ref, zn_ref, zg_ref, hs_ref, res_ref, cw_ref, cb_ref, wg_ref, bg_ref, lam_ref,
                    h0_ref, wo_ref, gm_ref, o_ref, hT_ref, a_ref, b_ref, h_ref, *, nb, n_blk, blk, steps):
    i = pl.program_id(0)
    last = pl.num_programs(0) - 1

    @pl.when(i == 0)
    def _():
        h_ref[...] = h0_ref[...]

    _lru_coeffs(zx_ref, zp_ref, zn_ref, cw_ref, cb_ref, wg_ref, bg_ref, lam_ref, a_ref, b_ref,
                i < last, i > 0, nb, n_blk, blk)

    def step(k, h):
        rows = pl.ds(pl.multiple_of((steps - 1 - k) * nb, nb), nb)
        h = a_ref[rows, :] * h + b_ref[rows, :]
        b_ref[rows, :] = hs_ref[rows, :] + h
        return h

    h_ref[...] = lax.fori_loop(0, steps, step, h_ref[...])

    @pl.when(i == last)
    def _():
        hT_ref[...] = h_ref[...]

    y = (b_ref[...] * _silu(zg_ref[...].astype(F32))).astype(BF16)
    proj = jnp.dot(y, wo_ref[...], preferred_element_type=F32)
    d = proj.shape[1]
    gated = (proj.reshape(steps, nb, d) * gm_ref[...][None]).reshape(steps * nb, d)
    o_ref[...] = res_ref[...] + gated


def _rglru(z, res, nb, w_rnn, conv_w, conv_b, wg, bg, lam, h0, w_out, gate):
    n_rows = z.shape[0]
    d = res.shape[1]
    t = n_rows // nb
    steps = min(t, LRU_STEPS)
    n_tiles = t // steps
    tile = steps * nb
    halo = 2 * nb
    n_halo = n_rows // halo
    per_tile = tile // halo
    n_blk, blk = wg.shape[1], wg.shape[2]
    assert nb % 8 == 0 and conv_w.shape[0] == 4

    kw = dict(nb=nb, n_blk=n_blk, blk=blk, steps=steps)
    const2 = lambda i: (0, 0)
    const3 = lambda i: (0, 0, 0)

    def specs(tile_of):
        return [pl.BlockSpec((tile, w_rnn), lambda i: (tile_of(i), 0)),
                pl.BlockSpec((halo, w_rnn), lambda i: (jnp.maximum(tile_of(i) * per_tile - 1, 0), 0)),
                pl.BlockSpec((halo, w_rnn), lambda i: (jnp.minimum((tile_of(i) + 1) * per_tile, n_halo - 1), 0))]

    param_specs = [pl.BlockSpec(conv_w.shape, const2), pl.BlockSpec((1, w_rnn), const2),
                   pl.BlockSpec((n_blk, blk, 2 * blk), const3), pl.BlockSpec((n_blk, 1, 2 * blk), const3),
                   pl.BlockSpec((1, w_rnn), const2), pl.BlockSpec((nb, w_rnn), const2)]
    scratch = [pltpu.VMEM((tile, w_rnn), F32), pltpu.VMEM((tile, w_rnn), F32), pltpu.VMEM((nb, w_rnn), F32)]

    fwd_tile = lambda i: i
    hs, h_fwd = pl.pallas_call(
        functools.partial(_lru_fwd_kernel, **kw),
        grid=(n_tiles,),
        in_specs=specs(fwd_tile) + param_specs,
        out_specs=[pl.BlockSpec((tile, w_rnn), lambda i: (i, 0)), pl.BlockSpec((nb, w_rnn), const2)],
        out_shape=[jax.ShapeDtypeStruct((n_rows, w_rnn), F32), jax.ShapeDtypeStruct((nb, w_rnn), F32)],
        scratch_shapes=scratch,
        compiler_params=_cparams("arbitrary"),
        name="rglru_forward",
    )(z, z, z, conv_w, conv_b.reshape(1, w_rnn), wg[0], bg[0], lam[0:1], h0[0])

    bwd_tile = lambda i: n_tiles - 1 - i
    out, h_bwd = pl.pallas_call(
        functools.partial(_lru_bwd_kernel, **kw),
        grid=(n_tiles,),
        in_specs=(specs(bwd_tile)
                  + [pl.BlockSpec((tile, w_rnn), lambda i: (bwd_tile(i), 1)),
                     pl.BlockSpec((tile, w_rnn), lambda i: (bwd_tile(i), 0)),
                     pl.BlockSpec((tile, d), lambda i: (bwd_tile(i), 0))]
                  + param_specs
                  + [pl.BlockSpec((w_rnn, d), const2), pl.BlockSpec((nb, d), const2)]),
        out_specs=[pl.BlockSpec((tile, d), lambda i: (bwd_tile(i), 0)), pl.BlockSpec((nb, w_rnn), const2)],
        out_shape=[jax.ShapeDtypeStruct((n_rows, d), F32), jax.ShapeDtypeStruct((nb, w_rnn), F32)],
        scratch_shapes=scratch,
        compiler_params=_cparams("arbitrary"),
        name="rglru_backward_out_proj",
    )(z, z, z, z, hs, res, conv_w, conv_b.reshape(1, w_rnn), wg[1], bg[1], lam[1:2], h0[1], w_out, gate)
    return out, jnp.stack([h_fwd, h_bwd])


def _final_norm_kernel(x_ref, g_ref, o_ref):
    x = x_ref[...]
    o_ref[...] = x * lax.rsqrt(jnp.mean(x * x, axis=-1, keepdims=True) + EPS) * g_ref[...]


def _final_norm(h_sbd, bsz, g):
    t = h_sbd.shape[0]
    d = g.shape[0]
    tm = min(t, 1024)
    return pl.pallas_call(
        _final_norm_kernel,
        grid=(bsz, t // tm),
        in_specs=[pl.BlockSpec((tm, d), lambda b, i: (i, b)), pl.BlockSpec((1, d), lambda b, i: (0, 0))],
        out_specs=pl.BlockSpec((None, tm, d), lambda b, i: (b, i, 0)),
        out_shape=jax.ShapeDtypeStruct((bsz, t, d), F32),
        compiler_params=_cparams("parallel", "parallel"),
        name="final_rmsnorm",
    )(h_sbd, g.reshape(1, d))


def kernel(x, c, ctx, c_ctx, norm_g, w_ada, b_ada, w_in_e, conv_a, rpb, w_out_e, w_in_o, conv_c, conv_c_b,
           w_ra, b_ra, w_ri, b_ri, lru_lam, w_out_o, final_g):
    bsz, s, d = x.shape
    lc = ctx.shape[1]
    depth = norm_g.shape[0]
    w_conv = conv_a.shape[2]
    n_heads = rpb.shape[1]
    win_r, win_c = (rpb.shape[2] + 1) // 2, (rpb.shape[3] + 1) // 2
    w_na = (w_in_e.shape[2] - 4 * w_conv) // 4
    dh = w_na // n_heads
    off_q = 4 * w_conv
    w_rnn = conv_c.shape[2]
    n_blk, blk = w_ra.shape[2], w_ra.shape[3]
    rows = s // GRID_W

    mod = _modulation(c, c_ctx, w_ada, b_ada)

    h_lat, lat_layout = x, "bsd"
    h_ctx, ctx_layout = ctx, "bsd"
    for layer in range(depth):
        need_ctx = layer < depth - 1
        m_lat = mod[layer, :bsz]
        m_ctx = jnp.broadcast_to(mod[layer, bsz], (bsz, 3 * d))
        shift, scale, gate = (m_lat[:, None, k * d:(k + 1) * d] for k in range(3))
        shift_c, scale_c, gate_c = (m_ctx[:, None, k * d:(k + 1) * d] for k in range(3))
        g = norm_g[layer]
        i = layer // 2
        if layer % 2 == 0:
            w_in = w_in_e[i].astype(BF16)
            w_out = w_out_e[i].astype(BF16)
            z = _in_proj(h_lat, lat_layout, bsz, g, scale, shift, w_in, "bsd")
            zc = _in_proj(h_ctx, ctx_layout, bsz, g, scale_c, shift_c, w_in, "bsd")
            bias = _na_bias_tables(rpb[i], rows, win_r, win_c)
            y_a = _conv_mix(z, conv_a[i], w_conv)
            y_b = _na_attn(z, zc, off_q, off_q + w_na, w_na, dh, bias)
            h_lat = _out_proj(y_a, y_b, w_out[:w_conv], w_out[w_conv:], h_lat, lat_layout, gate, "sbd")
            lat_layout = "sbd"
            if need_ctx:
                yc_a = _conv_mix(zc, conv_a[i], w_conv)
                yc_b = _ctx_attn(zc, off_q, w_na, dh)
                h_ctx = _out_proj(yc_a, yc_b, w_out[:w_conv], w_out[w_conv:], h_ctx, ctx_layout, gate_c, "sbd")
                ctx_layout = "sbd"
        else:
            assert lat_layout == "sbd" and ctx_layout == "sbd"
            w_in = w_in_o[i].astype(BF16)
            z = _in_proj(h_lat, "sbd", bsz, g, scale, shift, w_in, "sbd").reshape(s * bsz, 2 * w_rnn)
            zc = _in_proj(h_ctx, "sbd", bsz, g, scale_c, shift_c, w_in, "sbd").reshape(lc * bsz, 2 * w_rnn)
            wg = jnp.concatenate([w_ra[i], w_ri[i]], axis=-1).astype(BF16)
            bg = jnp.concatenate([b_ra[i].reshape(2, n_blk, 1, blk), b_ri[i].reshape(2, n_blk, 1, blk)], axis=-1)
            w_out = w_out_o[i].astype(BF16)
            h0 = jnp.zeros((2, bsz, w_rnn), F32)
            new_ctx, h_ctx_t = _rglru(zc, h_ctx.reshape(lc * bsz, d), bsz, w_rnn, conv_c[i], conv_c_b[i],
                                      wg, bg, lru_lam[i], h0, w_out, gate_c[:, 0])
            new_lat, _ = _rglru(z, h_lat.reshape(s * bsz, d), bsz, w_rnn, conv_c[i], conv_c_b[i],
                                wg, bg, lru_lam[i], h_ctx_t, w_out, gate[:, 0])
            h_lat = new_lat.reshape(s, bsz * d)
            if need_ctx:
                h_ctx = new_ctx.reshape(lc, bsz * d)
    assert lat_layout == "sbd"
    return _final_norm(h_lat, bsz, final_g)
```

```python
import functools

import numpy as np
import jax
import jax.numpy as jnp
from jax import lax
from jax.experimental import pallas as pl
from jax.experimental.pallas import tpu as pltpu

GRID_W = 64
EPS = 1e-6
LRU_C = 8.0
MASK_NEG = -1e30
LOG2E = 1.4426950408889634
ATTN_ROWS = 4
LANES = 128
VMEM_LIMIT_BYTES = 48 * 1024 * 1024

F32 = jnp.float32
BF16 = jnp.bfloat16


def _cparams(*sem):
    return pltpu.CompilerParams(dimension_semantics=sem, vmem_limit_bytes=VMEM_LIMIT_BYTES)


def _sigmoid(x):
    return 1.0 / (1.0 + jnp.exp(-x))


def _silu(x):
    return x * _sigmoid(x)


def _mod_kernel(s_ref, w_ref, b_ref, o_ref):
    a = _silu(s_ref[...])
    o_ref[...] = jnp.dot(a, w_ref[...], preferred_element_type=F32,
                         precision=lax.Precision.HIGHEST) + b_ref[...]


def _modulation(c, c_ctx, w_ada, b_ada):
    depth, d, n3 = w_ada.shape
    bsz = c.shape[0]
    rows = -(-(bsz + 1) // 8) * 8
    s_in = jnp.zeros((rows, d), F32).at[:bsz].set(c).at[bsz].set(c_ctx)
    tn = d
    return pl.pallas_call(
        _mod_kernel,
        grid=(depth, n3 // tn),
        in_specs=[pl.BlockSpec((rows, d), lambda l, j: (0, 0)),
                  pl.BlockSpec((None, d, tn), lambda l, j: (l, 0, j)),
                  pl.BlockSpec((None, 1, tn), lambda l, j: (l, 0, j))],
        out_specs=pl.BlockSpec((None, rows, tn), lambda l, j: (l, 0, j)),
        out_shape=jax.ShapeDtypeStruct((depth, rows, n3), F32),
        compiler_params=_cparams("parallel", "parallel"),
        name="adaln_mod",
    )(s_in, w_ada, b_ada.reshape(depth, 1, n3))


PROJ_STEPS = 128


def _in_proj_kernel(x_ref, g_ref, sc_ref, sh_ref, w_ref, o_ref, u_ref, *, nb, tm, in_tb, out_tb):
    d = x_ref.shape[-1]

    @pl.when(pl.program_id(1) == 0)
    def _():
        x = x_ref[...]
        y = x * lax.rsqrt(jnp.mean(x * x, axis=-1, keepdims=True) + EPS) * g_ref[...]
        sc = 1.0 + sc_ref[...]
        sh = sh_ref[...]
        if in_tb:
            u = y.reshape(tm, nb, d) * sc[None] + sh[None]
        else:
            u = y * sc[:, None, :] + sh[:, None, :]
        if in_tb != out_tb:
            u = jnp.swapaxes(u, 0, 1)
        u_ref[...] = u.reshape(tm * nb, d).astype(u_ref.dtype)

    z = jnp.dot(u_ref[...], w_ref[...], preferred_element_type=F32)
    o_ref[...] = z.reshape(o_ref.shape).astype(o_ref.dtype)


def _in_proj(h, h_layout, nb, g, scale, shift, w, out_layout):
    d, n = w.shape
    t = h.shape[1] if h_layout == "bsd" else h.shape[0] // nb
    tm = min(t, PROJ_STEPS)
    tn = min(n, 1024)
    if h_layout == "bsd":
        x_spec = pl.BlockSpec((nb, tm, d), lambda i, j: (0, i, 0))
    else:
        x_spec = pl.BlockSpec((tm * nb, d), lambda i, j: (i, 0))
    if out_layout == "bsd":
        o_spec = pl.BlockSpec((nb, tm, tn), lambda i, j: (0, i, j))
        o_shape = jax.ShapeDtypeStruct((nb, t, n), BF16)
    else:
        o_spec = pl.BlockSpec((tm * nb, tn), lambda i, j: (i, j))
        o_shape = jax.ShapeDtypeStruct((t * nb, n), BF16)
    vec_spec = pl.BlockSpec((nb, d), lambda i, j: (0, 0))
    return pl.pallas_call(
        functools.partial(_in_proj_kernel, nb=nb, tm=tm, in_tb=h_layout == "tb", out_tb=out_layout == "tb"),
        grid=(t // tm, n // tn),
        in_specs=[x_spec,
                  pl.BlockSpec((1, d), lambda i, j: (0, 0)),
                  vec_spec, vec_spec,
                  pl.BlockSpec((d, tn), lambda i, j: (0, j))],
        out_specs=o_spec,
        out_shape=o_shape,
        scratch_shapes=[pltpu.VMEM((tm * nb, d), BF16)],
        compiler_params=_cparams("parallel", "arbitrary"),
        name="norm_in_proj",
    )(h, g.reshape(1, d), scale, shift, w)


def _out_proj_kernel(ya_ref, yb_ref, wa_ref, wb_ref, h_ref, gt_ref, o_ref, *, nb, tm, in_tb):
    d = o_ref.shape[-1]
    ya = ya_ref[...].reshape(nb * tm, ya_ref.shape[-1])
    yb = yb_ref[...].reshape(nb * tm, yb_ref.shape[-1])
    proj = jnp.dot(ya, wa_ref[...], preferred_element_type=F32)
    proj += jnp.dot(yb, wb_ref[...], preferred_element_type=F32)
    gated = proj.reshape(nb, tm, d) * gt_ref[...][:, None, :]
    if in_tb:
        o_ref[...] = h_ref[...] + jnp.swapaxes(gated, 0, 1).reshape(tm * nb, d)
    else:
        o_ref[...] = jnp.swapaxes(h_ref[...] + gated, 0, 1).reshape(tm * nb, d)


def _out_proj(ya, yb, wa, wb, h, h_layout, gate):
    nb, t, wa_in = ya.shape
    wb_in = yb.shape[2]
    d = wa.shape[1]
    tm = min(t, PROJ_STEPS)
    if h_layout == "bsd":
        h_spec = pl.BlockSpec((nb, tm, d), lambda i: (0, i, 0))
    else:
        h_spec = pl.BlockSpec((tm * nb, d), lambda i: (i, 0))
    return pl.pallas_call(
        functools.partial(_out_proj_kernel, nb=nb, tm=tm, in_tb=h_layout == "tb"),
        grid=(t // tm,),
        in_specs=[pl.BlockSpec((nb, tm, wa_in), lambda i: (0, i, 0)),
                  pl.BlockSpec((nb, tm, wb_in), lambda i: (0, i, 0)),
                  pl.BlockSpec((wa_in, d), lambda i: (0, 0)),
                  pl.BlockSpec((wb_in, d), lambda i: (0, 0)),
                  h_spec,
                  pl.BlockSpec((nb, d), lambda i: (0, 0))],
        out_specs=pl.BlockSpec((tm * nb, d), lambda i: (i, 0)),
        out_shape=jax.ShapeDtypeStruct((t * nb, d), F32),
        compiler_params=_cparams("parallel"),
        name="out_proj_residual",
    )(ya, yb, wa, wb, h, gate)


CONV_CHUNK = 256
CONV_PAD = 8


def _conv_mix_kernel(zb_ref, zc_ref, zx_ref, zg_ref, w_ref, o_ref, p_ref, *, t, chunk):
    tc = o_ref.shape[1]
    pad = jnp.zeros((CONV_PAD, tc), F32)
    p_ref[0:CONV_PAD, :] = pad
    p_ref[t + CONV_PAD:t + 2 * CONV_PAD, :] = pad
    for s in range(0, t, chunk):
        p_ref[CONV_PAD + s:CONV_PAD + s + chunk, :] = (
            zc_ref[s:s + chunk, :].astype(F32) * zx_ref[s:s + chunk, :].astype(F32))
    w = w_ref[...]
    n_ext = chunk + 2 * CONV_PAD
    for s in range(0, t, chunk):
        ext = p_ref[s:s + n_ext, :]
        prev = pltpu.roll(ext, 1, axis=0)[CONV_PAD:CONV_PAD + chunk]
        nxt = pltpu.roll(ext, n_ext - 1, axis=0)[CONV_PAD:CONV_PAD + chunk]
        cur = ext[CONV_PAD:CONV_PAD + chunk]
        conv = w[0:1] * prev + w[1:2] * cur + w[2:3] * nxt
        y = zb_ref[s:s + chunk, :].astype(F32) * conv * _silu(zg_ref[s:s + chunk, :].astype(F32))
        o_ref[s:s + chunk, :] = y.astype(o_ref.dtype)


def _conv_mix(z, conv_w, w_conv):
    bsz, t, _ = z.shape
    assert conv_w.shape[0] == 3
    tc = min(w_conv, 256)
    nc = w_conv // tc
    chunk = min(t, CONV_CHUNK)

    def zspec(group):
        return pl.BlockSpec((None, t, tc), lambda b, j, group=group: (b, 0, group * nc + j))

    return pl.pallas_call(
        functools.partial(_conv_mix_kernel, t=t, chunk=chunk),
        grid=(bsz, nc),
        in_specs=[zspec(0), zspec(1), zspec(2), zspec(3),
                  pl.BlockSpec((3, tc), lambda b, j: (0, j))],
        out_specs=pl.BlockSpec((None, t, tc), lambda b, j: (b, 0, j)),
        out_shape=jax.ShapeDtypeStruct((bsz, t, w_conv), BF16),
        scratch_shapes=[pltpu.VMEM((t + 2 * CONV_PAD, tc), F32)],
        compiler_params=_cparams("parallel", "parallel"),
        name="gated_conv3",
    )(z, z, z, z, conv_w)


def _head_masks(dh):
    lane = lax.broadcasted_iota(jnp.int32, (1, LANES), 1)
    return [(lane >= i * dh) & (lane < (i + 1) * dh) for i in range(LANES // dh)]


def _row_reduce(op, reduce_fn, *arrays):
    tiles = [x[:, j:j + LANES] for x in arrays if x.shape[1] % LANES == 0 for j in range(0, x.shape[1], LANES)]
    parts = [x for x in arrays if x.shape[1] % LANES != 0]
    while len(tiles) > 1:
        tiles = [op(a, b) for a, b in zip(tiles[0::2], tiles[1::2])] + (tiles[-1:] if len(tiles) % 2 else [])
    out = None
    for x in tiles + parts:
        r = reduce_fn(x, axis=-1, keepdims=True)
        out = r if out is None else op(out, r)
    return out


def _dot_nt(a, b):
    return lax.dot_general(a, b, (((1,), (1,)), ((), ())), preferred_element_type=F32)


def _ctx_attn_kernel(q_ref, k_ref, v_ref, g_ref, o_ref, *, dh):
    masks = _head_masks(dh)
    q = q_ref[...]
    k = k_ref[...]
    v = v_ref[...]
    out = None
    for msk in masks:
        qh = jnp.where(msk, q, jnp.zeros_like(q))
        s = _dot_nt(qh, k)
        p = jnp.exp2(s - jnp.max(s, axis=-1, keepdims=True))
        o = jnp.dot(p.astype(BF16), v, preferred_element_type=F32) / jnp.sum(p, axis=-1, keepdims=True)
        out = o if out is None else jnp.where(msk, o, out)
    o_ref[...] = (out * _silu(g_ref[...].astype(F32))).astype(o_ref.dtype)


def _ctx_attn(zc, off_q, w_na, dh):
    bsz, lc, _ = zc.shape
    npair = w_na // LANES

    def spec(off):
        return pl.BlockSpec((None, lc, LANES), lambda b, p, off=off: (b, 0, off // LANES + p))

    return pl.pallas_call(
        functools.partial(_ctx_attn_kernel, dh=dh),
        grid=(bsz, npair),
        in_specs=[spec(off_q), spec(off_q + w_na), spec(off_q + 2 * w_na), spec(off_q + 3 * w_na)],
        out_specs=pl.BlockSpec((None, lc, LANES), lambda b, p: (b, 0, p)),
        out_shape=jax.ShapeDtypeStruct((bsz, lc, w_na), BF16),
        compiler_params=_cparams("parallel", "parallel"),
        name="context_attention",
    )(zc, zc, zc, zc)


def _na_attn_kernel(q_ref, k_ref, v_ref, g_ref, kc_ref, vc_ref, bias_ref, o_ref, *,
                    dh, n_groups, rows, win_r):
    masks = _head_masks(dh)
    rq = ATTN_ROWS * GRID_W
    wrows = ATTN_ROWS + win_r
    rk = wrows * GRID_W
    kc = kc_ref[...]
    vc = vc_ref[...]

    def group(g, carry):
        q0 = pl.multiple_of(g * rq, rq)
        ws = jnp.clip(g * ATTN_ROWS - win_r // 2, 0, rows - wrows)
        k0 = pl.multiple_of(ws * GRID_W, GRID_W)
        var = jnp.where(g == 0, 0, jnp.where(g == n_groups - 1, 2, 1))
        q = q_ref[pl.ds(q0, rq), :]
        kw = k_ref[pl.ds(k0, rk), :]
        vw = v_ref[pl.ds(k0, rk), :]
        out = None
        for hi, msk in enumerate(masks):
            qh = jnp.where(msk, q, jnp.zeros_like(q))
            s_w = _dot_nt(qh, kw) + bias_ref[hi, var]
            s_c = _dot_nt(qh, kc)
            m = _row_reduce(jnp.maximum, jnp.max, s_w, s_c)
            p_w = jnp.exp2(s_w - m)
            p_c = jnp.exp2(s_c - m)
            l = _row_reduce(jnp.add, jnp.sum, p_w, p_c)
            o = (jnp.dot(p_w.astype(BF16), vw, preferred_element_type=F32)
                 + jnp.dot(p_c.astype(BF16), vc, preferred_element_type=F32)) / l
            out = o if out is None else jnp.where(msk, o, out)
        gt = g_ref[pl.ds(q0, rq), :].astype(F32)
        o_ref[pl.ds(q0, rq), :] = (out * _silu(gt)).astype(o_ref.dtype)
        return carry

    lax.fori_loop(0, n_groups, group, 0, unroll=2)


def _na_bias_tables(rpb, rows, win_r, win_c):
    n_heads = rpb.shape[0]
    r_grp = ATTN_ROWS
    wrows = r_grp + win_r
    cols = np.arange(GRID_W)
    cs = np.clip(cols - win_c // 2, 0, GRID_W - win_c)
    col_ok = (cols[None, :] >= cs[:, None]) & (cols[None, :] < cs[:, None] + win_c)
    padded = jnp.pad(rpb.astype(F32) * LOG2E, ((0, 0), (0, 0), (GRID_W - win_c, GRID_W - win_c)))
    toep = jnp.stack([padded[:, :, GRID_W - 1 - c:2 * GRID_W - 1 - c] for c in range(GRID_W)], axis=2)
    toep = jnp.where(col_ok[None, None], toep, MASK_NEG)
    n_dr = 2 * win_r - 1
    toep = jnp.concatenate([toep, jnp.full((n_heads, 1, GRID_W, GRID_W), MASK_NEG, F32)], axis=1)
    tile_idx = np.full((3, r_grp, wrows), n_dr, np.int32)
    for v, r0 in enumerate((0, r_grp, rows - r_grp)):
        ws = int(np.clip(r0 - win_r // 2, 0, rows - wrows))
        for i in range(r_grp):
            r = r0 + i
            rs = int(np.clip(r - win_r // 2, 0, rows - win_r))
            for j in range(wrows):
                rp = ws + j
                if rs <= rp < rs + win_r:
                    tile_idx[v, i, j] = rp - r + win_r - 1
    tab = jnp.take(toep, tile_idx.reshape(-1), axis=1)
    tab = tab.reshape(n_heads, 3, r_grp, wrows, GRID_W, GRID_W).transpose(0, 1, 2, 4, 3, 5)
    return tab.reshape(n_heads, 3, r_grp * GRID_W, wrows * GRID_W)


def _na_attn(z, zc, off_q, off_kc, w_na, dh, bias):
    bsz, s, _ = z.shape
    lc = zc.shape[1]
    npair = w_na // LANES
    rows = s // GRID_W
    win_r = bias.shape[3] // GRID_W - ATTN_ROWS
    assert rows % ATTN_ROWS == 0 and rows >= ATTN_ROWS + win_r and ATTN_ROWS >= win_r // 2
    assert LANES % dh == 0
    hp = LANES // dh
    n_groups = rows // ATTN_ROWS

    def zspec(off):
        return pl.BlockSpec((None, s, LANES), lambda p, b, off=off: (b, 0, off // LANES + p))

    def cspec(off):
        return pl.BlockSpec((None, lc, LANES), lambda p, b, off=off: (b, 0, off // LANES + p))

    return pl.pallas_call(
        functools.partial(_na_attn_kernel, dh=dh, n_groups=n_groups, rows=rows, win_r=win_r),
        grid=(npair, bsz),
        in_specs=[zspec(off_q), zspec(off_q + w_na), zspec(off_q + 2 * w_na), zspec(off_q + 3 * w_na),
                  cspec(off_kc), cspec(off_kc + w_na),
                  pl.BlockSpec((hp,) + bias.shape[1:], lambda p, b: (p, 0, 0, 0))],
        out_specs=pl.BlockSpec((None, s, LANES), lambda p, b: (b, 0, p)),
        out_shape=jax.ShapeDtypeStruct((bsz, s, w_na), BF16),
        compiler_params=_cparams("parallel", "parallel"),
        name="neighbourhood_attention",
    )(z, z, z, z, zc, zc, bias)


LRU_STEPS = 32


def _lru_coeffs(zx_ref, zp_ref, zn_ref, cw_ref, cb_ref, wg_ref, bg_ref, lam_ref, a_ref, b_ref,
                has_prev, has_next, nb, n_blk, blk):
    n = zx_ref.shape[0]
    k_taps = cw_ref.shape[0]
    left = (k_taps - 1) // 2

    def block(j, carry):
        c0 = pl.multiple_of(j * blk, blk)
        cols = pl.ds(c0, blk)
        x = zx_ref[:, cols].astype(F32)
        xp = jnp.where(has_prev, zp_ref[:, cols].astype(F32), 0.0)[2 * nb - left * nb:]
        xn = jnp.where(has_next, zn_ref[:, cols].astype(F32), 0.0)
        xe = jnp.concatenate([xp, x, xn], axis=0)
        cw = cw_ref[:, cols]
        xl = cb_ref[:, cols]
        for k in range(k_taps):
            xl = xl + cw[k:k + 1] * xe[k * nb:k * nb + n]
        tg = jnp.tanh(jnp.dot(xl.astype(BF16), wg_ref[j], preferred_element_type=F32) + bg_ref[j])
        half_c = (0.5 * LRU_C) * jax.nn.log_sigmoid(lam_ref[:, cols])
        log_a = half_c + half_c * tg[:, :blk]
        ig = 0.5 + 0.5 * tg[:, blk:]
        a_ref[:, cols] = jnp.exp(log_a)
        th = jnp.tanh(log_a)
        sq = (-2.0 * th) * (1.0 / (1.0 - th))
        root = jnp.where(sq > 0.0, sq * lax.rsqrt(sq), 0.0)
        b_ref[:, cols] = root * ig * xl
        return carry

    lax.fori_loop(0, n_blk, block, 0)


def _lru_fwd_kernel(zx_ref, zp_ref, zn_ref, cw_ref, cb_ref, wg_ref, bg_ref, lam_ref, h0_ref,
                    hs_ref, hT_ref, a_ref, b_ref, h_ref, *, nb, n_blk, blk, steps):
    i = pl.program_id(0)
    last = pl.num_programs(0) - 1

    @pl.when(i == 0)
    def _():
        h_ref[...] = h0_ref[...]

    _lru_coeffs(zx_ref, zp_ref, zn_ref, cw_ref, cb_ref, wg_ref, bg_ref, lam_ref, a_ref, b_ref,
                i > 0, i < last, nb, n_blk, blk)

    def step(t, h):
        rows = pl.ds(pl.multiple_of(t * nb, nb), nb)
        h = a_ref[rows, :] * h + b_ref[rows, :]
        hs_ref[rows, :] = h
        return h

    h_ref[...] = lax.fori_loop(0, steps, step, h_ref[...])

    @pl.when(i == last)
    def _():
        hT_ref[...] = h_ref[...]


def _lru_bwd_kernel(zx_ref, zp_ref, zn_ref, zg_ref, hs_ref, res_ref, cw_ref, cb_ref, wg_ref, bg_ref, lam_ref,
                    h0_ref, wo_ref, gm_ref, o_ref, hT_ref, a_ref, b_ref, h_ref, *, nb, n_blk, blk, steps):
    i = pl.program_id(0)
    last = pl.num_programs(0) - 1

    @pl.when(i == 0)
    def _():
        h_ref[...] = h0_ref[...]

    _lru_coeffs(zx_ref, zp_ref, zn_ref, cw_ref, cb_ref, wg_ref, bg_ref, lam_ref, a_ref, b_ref,
                i < last, i > 0, nb, n_blk, blk)

    def step(k, h):
        rows = pl.ds(pl.multiple_of((steps - 1 - k) * nb, nb), nb)
        h = a_ref[rows, :] * h + b_ref[rows, :]
        b_ref[rows, :] = hs_ref[rows, :] + h
        return h

    h_ref[...] = lax.fori_loop(0, steps, step, h_ref[...])

    @pl.when(i == last)
    def _():
        hT_ref[...] = h_ref[...]

    y = (b_ref[...] * _silu(zg_ref[...].astype(F32))).astype(BF16)
    proj = jnp.dot(y, wo_ref[...], preferred_element_type=F32)
    d = proj.shape[1]
    gated = (proj.reshape(steps, nb, d) * gm_ref[...][None]).reshape(steps * nb, d)
    o_ref[...] = res_ref[...] + gated


def _rglru(z, res, nb, w_rnn, conv_w, conv_b, wg, bg, lam, h0, w_out, gate):
    n_rows = z.shape[0]
    d = res.shape[1]
    t = n_rows // nb
    steps = min(t, LRU_STEPS)
    n_tiles = t // steps
    tile = steps * nb
    halo = 2 * nb
    n_halo = n_rows // halo
    per_tile = tile // halo
    n_blk, blk = wg.shape[1], wg.shape[2]
    assert nb % 8 == 0 and conv_w.shape[0] == 4

    kw = dict(nb=nb, n_blk=n_blk, blk=blk, steps=steps)
    const2 = lambda i: (0, 0)
    const3 = lambda i: (0, 0, 0)

    def specs(tile_of):
        return [pl.BlockSpec((tile, w_rnn), lambda i: (tile_of(i), 0)),
                pl.BlockSpec((halo, w_rnn), lambda i: (jnp.maximum(tile_of(i) * per_tile - 1, 0), 0)),
                pl.BlockSpec((halo, w_rnn), lambda i: (jnp.minimum((tile_of(i) + 1) * per_tile, n_halo - 1), 0))]

    param_specs = [pl.BlockSpec(conv_w.shape, const2), pl.BlockSpec((1, w_rnn), const2),
                   pl.BlockSpec((n_blk, blk, 2 * blk), const3), pl.BlockSpec((n_blk, 1, 2 * blk), const3),
                   pl.BlockSpec((1, w_rnn), const2), pl.BlockSpec((nb, w_rnn), const2)]
    scratch = [pltpu.VMEM((tile, w_rnn), F32), pltpu.VMEM((tile, w_rnn), F32), pltpu.VMEM((nb, w_rnn), F32)]

    fwd_tile = lambda i: i
    hs, h_fwd = pl.pallas_call(
        functools.partial(_lru_fwd_kernel, **kw),
        grid=(n_tiles,),
        in_specs=specs(fwd_tile) + param_specs,
        out_specs=[pl.BlockSpec((tile, w_rnn), lambda i: (i, 0)), pl.BlockSpec((nb, w_rnn), const2)],
        out_shape=[jax.ShapeDtypeStruct((n_rows, w_rnn), F32), jax.ShapeDtypeStruct((nb, w_rnn), F32)],
        scratch_shapes=scratch,
        compiler_params=_cparams("arbitrary"),
        name="rglru_forward",
    )(z, z, z, conv_w, conv_b.reshape(1, w_rnn), wg[0], bg[0], lam[0:1], h0[0])

    bwd_tile = lambda i: n_tiles - 1 - i
    out, h_bwd = pl.pallas_call(
        functools.partial(_lru_bwd_kernel, **kw),
        grid=(n_tiles,),
        in_specs=(specs(bwd_tile)
                  + [pl.BlockSpec((tile, w_rnn), lambda i: (bwd_tile(i), 1)),
                     pl.BlockSpec((tile, w_rnn), lambda i: (bwd_tile(i), 0)),
                     pl.BlockSpec((tile, d), lambda i: (bwd_tile(i), 0))]
                  + param_specs
                  + [pl.BlockSpec((w_rnn, d), const2), pl.BlockSpec((nb, d), const2)]),
        out_specs=[pl.BlockSpec((tile, d), lambda i: (bwd_tile(i), 0)), pl.BlockSpec((nb, w_rnn), const2)],
        out_shape=[jax.ShapeDtypeStruct((n_rows, d), F32), jax.ShapeDtypeStruct((nb, w_rnn), F32)],
        scratch_shapes=scratch,
        compiler_params=_cparams("arbitrary"),
        name="rglru_backward_out_proj",
    )(z, z, z, z, hs, res, conv_w, conv_b.reshape(1, w_rnn), wg[1], bg[1], lam[1:2], h0[1], w_out, gate)
    return out, jnp.stack([h_fwd, h_bwd])


def _final_norm_kernel(x_ref, g_ref, o_ref):
    nb, tm, d = o_ref.shape
    x = x_ref[...]
    y = x * lax.rsqrt(jnp.mean(x * x, axis=-1, keepdims=True) + EPS) * g_ref[...]
    o_ref[...] = jnp.swapaxes(y.reshape(tm, nb, d), 0, 1)


def _final_norm(h_tb, nb, g):
    d = g.shape[0]
    t = h_tb.shape[0] // nb
    tm = min(t, PROJ_STEPS)
    return pl.pallas_call(
        _final_norm_kernel,
        grid=(t // tm,),
        in_specs=[pl.BlockSpec((tm * nb, d), lambda i: (i, 0)), pl.BlockSpec((1, d), lambda i: (0, 0))],
        out_specs=pl.BlockSpec((nb, tm, d), lambda i: (0, i, 0)),
        out_shape=jax.ShapeDtypeStruct((nb, t, d), F32),
        compiler_params=_cparams("parallel"),
        name="final_rmsnorm",
    )(h_tb, g.reshape(1, d))


def kernel(x, c, ctx, c_ctx, norm_g, w_ada, b_ada, w_in_e, conv_a, rpb, w_out_e, w_in_o, conv_c, conv_c_b,
           w_ra, b_ra, w_ri, b_ri, lru_lam, w_out_o, final_g):
    bsz, s, d = x.shape
    lc = ctx.shape[1]
    depth = norm_g.shape[0]
    w_conv = conv_a.shape[2]
    n_heads = rpb.shape[1]
    win_r, win_c = (rpb.shape[2] + 1) // 2, (rpb.shape[3] + 1) // 2
    w_na = (w_in_e.shape[2] - 4 * w_conv) // 4
    dh = w_na // n_heads
    off_q = 4 * w_conv
    w_rnn = conv_c.shape[2]
    n_blk, blk = w_ra.shape[2], w_ra.shape[3]
    rows = s // GRID_W

    mod = _modulation(c, c_ctx, w_ada, b_ada)

    h_lat, lat_layout = x, "bsd"
    h_ctx, ctx_layout = ctx, "bsd"
    for layer in range(depth):
        need_ctx = layer < depth - 1
        m_lat = mod[layer, :bsz]
        m_ctx = jnp.broadcast_to(mod[layer, bsz], (bsz, 3 * d))
        shift, scale, gate = (m_lat[:, k * d:(k + 1) * d] for k in range(3))
        shift_c, scale_c, gate_c = (m_ctx[:, k * d:(k + 1) * d] for k in range(3))
        g = norm_g[layer]
        i = layer // 2
        if layer % 2 == 0:
            q_scale = jnp.ones((w_in_e.shape[2],), F32).at[off_q:off_q + w_na].set(dh ** -0.5 * LOG2E)
            w_in = (w_in_e[i] * q_scale).astype(BF16)
            w_out = w_out_e[i].astype(BF16)
            z = _in_proj(h_lat, lat_layout, bsz, g, scale, shift, w_in, "bsd")
            zc = _in_proj(h_ctx, ctx_layout, bsz, g, scale_c, shift_c, w_in, "bsd")
            bias = _na_bias_tables(rpb[i], rows, win_r, win_c)
            y_a = _conv_mix(z, conv_a[i], w_conv)
            y_b = _na_attn(z, zc, off_q, off_q + w_na, w_na, dh, bias)
            h_lat = _out_proj(y_a, y_b, w_out[:w_conv], w_out[w_conv:], h_lat, lat_layout, gate)
            lat_layout = "tb"
            if need_ctx:
                yc_a = _conv_mix(zc, conv_a[i], w_conv)
                yc_b = _ctx_attn(zc, off_q, w_na, dh)
                h_ctx = _out_proj(yc_a, yc_b, w_out[:w_conv], w_out[w_conv:], h_ctx, ctx_layout, gate_c)
                ctx_layout = "tb"
        else:
            assert lat_layout == "tb" and ctx_layout == "tb"
            w_in = w_in_o[i].astype(BF16)
            z = _in_proj(h_lat, "tb", bsz, g, scale, shift, w_in, "tb")
            zc = _in_proj(h_ctx, "tb", bsz, g, scale_c, shift_c, w_in, "tb")
            wg = (0.5 * jnp.concatenate([w_ra[i], w_ri[i]], axis=-1)).astype(BF16)
            bg = 0.5 * jnp.concatenate([b_ra[i].reshape(2, n_blk, 1, blk), b_ri[i].reshape(2, n_blk, 1, blk)], axis=-1)
            w_out = w_out_o[i].astype(BF16)
            h0 = jnp.zeros((2, bsz, w_rnn), F32)
            new_ctx, h_ctx_t = _rglru(zc, h_ctx, bsz, w_rnn, conv_c[i], conv_c_b[i],
                                      wg, bg, lru_lam[i], h0, w_out, gate_c)
            h_lat, _ = _rglru(z, h_lat, bsz, w_rnn, conv_c[i], conv_c_b[i],
                              wg, bg, lru_lam[i], h_ctx_t, w_out, gate)
            if need_ctx:
                h_ctx = new_ctx
    assert lat_layout == "tb"
    return _final_norm(h_lat, bsz, final_g)
```

```python
import functools

import numpy as np
import jax
import jax.numpy as jnp
from jax import lax
from jax.experimental import pallas as pl
from jax.experimental.pallas import tpu as pltpu

GRID_W = 64
EPS = 1e-6
LRU_C = 8.0
MASK_NEG = -1e30
LOG2E = 1.4426950408889634
ATTN_ROWS = 4
LANES = 128
VMEM_LIMIT_BYTES = 48 * 1024 * 1024

F32 = jnp.float32
BF16 = jnp.bfloat16


def _cparams(*sem):
    return pltpu.CompilerParams(dimension_semantics=sem, vmem_limit_bytes=VMEM_LIMIT_BYTES)


def _sigmoid(x):
    return 1.0 / (1.0 + jnp.exp(-x))


def _silu(x):
    return x * _sigmoid(x)


def _mod_kernel(s_ref, w_ref, b_ref, o_ref):
    a = _silu(s_ref[...])
    o_ref[...] = jnp.dot(a, w_ref[...], preferred_element_type=F32,
                         precision=lax.Precision.HIGHEST) + b_ref[...]


def _modulation(c, c_ctx, w_ada, b_ada):
    depth, d, n3 = w_ada.shape
    bsz = c.shape[0]
    rows = -(-(bsz + 1) // 8) * 8
    s_in = jnp.zeros((rows, d), F32).at[:bsz].set(c).at[bsz].set(c_ctx)
    tn = d
    return pl.pallas_call(
        _mod_kernel,
        grid=(depth, n3 // tn),
        in_specs=[pl.BlockSpec((rows, d), lambda l, j: (0, 0)),
                  pl.BlockSpec((None, d, tn), lambda l, j: (l, 0, j)),
                  pl.BlockSpec((None, 1, tn), lambda l, j: (l, 0, j))],
        out_specs=pl.BlockSpec((None, rows, tn), lambda l, j: (l, 0, j)),
        out_shape=jax.ShapeDtypeStruct((depth, rows, n3), F32),
        compiler_params=_cparams("parallel", "parallel"),
        name="adaln_mod",
    )(s_in, w_ada, b_ada.reshape(depth, 1, n3))


PROJ_STEPS = 128
PROJ_COLS = 2048


def _in_proj_kernel(x_ref, g_ref, sc_ref, sh_ref, w_ref, o_ref, u_ref, *, nb, tm, in_tb, out_tb):
    d = x_ref.shape[-1]

    @pl.when(pl.program_id(1) == 0)
    def _():
        x = x_ref[...]
        y = x * lax.rsqrt(jnp.mean(x * x, axis=-1, keepdims=True) + EPS) * g_ref[...]
        sc = 1.0 + sc_ref[...]
        sh = sh_ref[...]
        if in_tb:
            u = y.reshape(tm, nb, d) * sc[None] + sh[None]
        else:
            u = y * sc[:, None, :] + sh[:, None, :]
        if in_tb != out_tb:
            u = jnp.swapaxes(u, 0, 1)
        u_ref[...] = u.reshape(tm * nb, d).astype(u_ref.dtype)

    z = jnp.dot(u_ref[...], w_ref[...], preferred_element_type=F32)
    o_ref[...] = z.reshape(o_ref.shape).astype(o_ref.dtype)


def _in_proj(h, h_layout, nb, g, scale, shift, w, out_layout):
    d, n = w.shape
    t = h.shape[1] if h_layout == "bsd" else h.shape[0] // nb
    tm = min(t, PROJ_STEPS)
    tn = min(n, PROJ_COLS)
    if h_layout == "bsd":
        x_spec = pl.BlockSpec((nb, tm, d), lambda i, j: (0, i, 0))
    else:
        x_spec = pl.BlockSpec((tm * nb, d), lambda i, j: (i, 0))
    if out_layout == "bsd":
        o_spec = pl.BlockSpec((nb, tm, tn), lambda i, j: (0, i, j))
        o_shape = jax.ShapeDtypeStruct((nb, t, n), BF16)
    else:
        o_spec = pl.BlockSpec((tm * nb, tn), lambda i, j: (i, j))
        o_shape = jax.ShapeDtypeStruct((t * nb, n), BF16)
    vec_spec = pl.BlockSpec((nb, d), lambda i, j: (0, 0))
    return pl.pallas_call(
        functools.partial(_in_proj_kernel, nb=nb, tm=tm, in_tb=h_layout == "tb", out_tb=out_layout == "tb"),
        grid=(t // tm, n // tn),
        in_specs=[x_spec,
                  pl.BlockSpec((1, d), lambda i, j: (0, 0)),
                  vec_spec, vec_spec,
                  pl.BlockSpec((d, tn), lambda i, j: (0, j))],
        out_specs=o_spec,
        out_shape=o_shape,
        scratch_shapes=[pltpu.VMEM((tm * nb, d), BF16)],
        compiler_params=_cparams("parallel", "arbitrary"),
        name="norm_in_proj",
    )(h, g.reshape(1, d), scale, shift, w)


def _out_proj_kernel(ya_ref, yb_ref, wa_ref, wb_ref, h_ref, gt_ref, o_ref, *, nb, tm, in_tb):
    d = o_ref.shape[-1]
    ya = ya_ref[...].reshape(nb * tm, ya_ref.shape[-1])
    yb = yb_ref[...].reshape(nb * tm, yb_ref.shape[-1])
    proj = jnp.dot(ya, wa_ref[...], preferred_element_type=F32)
    proj += jnp.dot(yb, wb_ref[...], preferred_element_type=F32)
    gated = proj.reshape(nb, tm, d) * gt_ref[...][:, None, :]
    if in_tb:
        o_ref[...] = h_ref[...] + jnp.swapaxes(gated, 0, 1).reshape(tm * nb, d)
    else:
        o_ref[...] = jnp.swapaxes(h_ref[...] + gated, 0, 1).reshape(tm * nb, d)


def _out_proj(ya, yb, wa, wb, h, h_layout, gate):
    nb, t, wa_in = ya.shape
    wb_in = yb.shape[2]
    d = wa.shape[1]
    tm = min(t, PROJ_STEPS)
    if h_layout == "bsd":
        h_spec = pl.BlockSpec((nb, tm, d), lambda i: (0, i, 0))
    else:
        h_spec = pl.BlockSpec((tm * nb, d), lambda i: (i, 0))
    return pl.pallas_call(
        functools.partial(_out_proj_kernel, nb=nb, tm=tm, in_tb=h_layout == "tb"),
        grid=(t // tm,),
        in_specs=[pl.BlockSpec((nb, tm, wa_in), lambda i: (0, i, 0)),
                  pl.BlockSpec((nb, tm, wb_in), lambda i: (0, i, 0)),
                  pl.BlockSpec((wa_in, d), lambda i: (0, 0)),
                  pl.BlockSpec((wb_in, d), lambda i: (0, 0)),
                  h_spec,
                  pl.BlockSpec((nb, d), lambda i: (0, 0))],
        out_specs=pl.BlockSpec((tm * nb, d), lambda i: (i, 0)),
        out_shape=jax.ShapeDtypeStruct((t * nb, d), F32),
        compiler_params=_cparams("parallel"),
        name="out_proj_residual",
    )(ya, yb, wa, wb, h, gate)


CONV_CHUNK = 256
CONV_PAD = 8


def _conv_mix_kernel(zb_ref, zc_ref, zx_ref, zg_ref, w_ref, o_ref, p_ref, *, t, chunk):
    tc = o_ref.shape[1]
    pad = jnp.zeros((CONV_PAD, tc), F32)
    p_ref[0:CONV_PAD, :] = pad
    p_ref[t + CONV_PAD:t + 2 * CONV_PAD, :] = pad
    for s in range(0, t, chunk):
        p_ref[CONV_PAD + s:CONV_PAD + s + chunk, :] = (
            zc_ref[s:s + chunk, :].astype(F32) * zx_ref[s:s + chunk, :].astype(F32))
    w = w_ref[...]
    n_ext = chunk + 2 * CONV_PAD
    for s in range(0, t, chunk):
        ext = p_ref[s:s + n_ext, :]
        prev = pltpu.roll(ext, 1, axis=0)[CONV_PAD:CONV_PAD + chunk]
        nxt = pltpu.roll(ext, n_ext - 1, axis=0)[CONV_PAD:CONV_PAD + chunk]
        cur = ext[CONV_PAD:CONV_PAD + chunk]
        conv = w[0:1] * prev + w[1:2] * cur + w[2:3] * nxt
        y = zb_ref[s:s + chunk, :].astype(F32) * conv * _silu(zg_ref[s:s + chunk, :].astype(F32))
        o_ref[s:s + chunk, :] = y.astype(o_ref.dtype)


def _conv_mix(z, conv_w, w_conv):
    bsz, t, _ = z.shape
    assert conv_w.shape[0] == 3
    tc = min(w_conv, 256)
    nc = w_conv // tc
    chunk = min(t, CONV_CHUNK)

    def zspec(group):
        return pl.BlockSpec((None, t, tc), lambda b, j, group=group: (b, 0, group * nc + j))

    return pl.pallas_call(
        functools.partial(_conv_mix_kernel, t=t, chunk=chunk),
        grid=(bsz, nc),
        in_specs=[zspec(0), zspec(1), zspec(2), zspec(3),
                  pl.BlockSpec((3, tc), lambda b, j: (0, j))],
        out_specs=pl.BlockSpec((None, t, tc), lambda b, j: (b, 0, j)),
        out_shape=jax.ShapeDtypeStruct((bsz, t, w_conv), BF16),
        scratch_shapes=[pltpu.VMEM((t + 2 * CONV_PAD, tc), F32)],
        compiler_params=_cparams("parallel", "parallel"),
        name="gated_conv3",
    )(z, z, z, z, conv_w)


def _head_masks(dh):
    lane = lax.broadcasted_iota(jnp.int32, (1, LANES), 1)
    return [(lane >= i * dh) & (lane < (i + 1) * dh) for i in range(LANES // dh)]


def _row_reduce(op, reduce_fn, *arrays):
    tiles = [x[:, j:j + LANES] for x in arrays if x.shape[1] % LANES == 0 for j in range(0, x.shape[1], LANES)]
    parts = [x for x in arrays if x.shape[1] % LANES != 0]
    while len(tiles) > 1:
        tiles = [op(a, b) for a, b in zip(tiles[0::2], tiles[1::2])] + (tiles[-1:] if len(tiles) % 2 else [])
    out = None
    for x in tiles + parts:
        r = reduce_fn(x, axis=-1, keepdims=True)
        out = r if out is None else op(out, r)
    return out


def _dot_nt(a, b):
    return lax.dot_general(a, b, (((1,), (1,)), ((), ())), preferred_element_type=F32)


def _ctx_attn_kernel(q_ref, k_ref, v_ref, g_ref, o_ref, *, dh):
    masks = _head_masks(dh)
    q = q_ref[...]
    k = k_ref[...]
    v = v_ref[...]
    out = None
    for msk in masks:
        qh = jnp.where(msk, q, jnp.zeros_like(q))
        s = _dot_nt(qh, k)
        p = jnp.exp2(s - jnp.max(s, axis=-1, keepdims=True))
        o = jnp.dot(p.astype(BF16), v, preferred_element_type=F32) / jnp.sum(p, axis=-1, keepdims=True)
        out = o if out is None else jnp.where(msk, o, out)
    o_ref[...] = (out * _silu(g_ref[...].astype(F32))).astype(o_ref.dtype)


def _ctx_attn(zc, off_q, w_na, dh):
    bsz, lc, _ = zc.shape
    npair = w_na // LANES

    def spec(off):
        return pl.BlockSpec((None, lc, LANES), lambda b, p, off=off: (b, 0, off // LANES + p))

    return pl.pallas_call(
        functools.partial(_ctx_attn_kernel, dh=dh),
        grid=(bsz, npair),
        in_specs=[spec(off_q), spec(off_q + w_na), spec(off_q + 2 * w_na), spec(off_q + 3 * w_na)],
        out_specs=pl.BlockSpec((None, lc, LANES), lambda b, p: (b, 0, p)),
        out_shape=jax.ShapeDtypeStruct((bsz, lc, w_na), BF16),
        compiler_params=_cparams("parallel", "parallel"),
        name="context_attention",
    )(zc, zc, zc, zc)


def _na_attn_kernel(q_ref, k_ref, v_ref, g_ref, kc_ref, vc_ref, bias_ref, o_ref, *,
                    dh, n_groups, rows, win_r):
    masks = _head_masks(dh)
    rq = ATTN_ROWS * GRID_W
    wrows = ATTN_ROWS + win_r
    rk = wrows * GRID_W
    kc = kc_ref[...]
    vc = vc_ref[...]

    def group(g, carry):
        q0 = pl.multiple_of(g * rq, rq)
        ws = jnp.clip(g * ATTN_ROWS - win_r // 2, 0, rows - wrows)
        k0 = pl.multiple_of(ws * GRID_W, GRID_W)
        var = jnp.where(g == 0, 0, jnp.where(g == n_groups - 1, 2, 1))
        q = q_ref[pl.ds(q0, rq), :]
        kw = k_ref[pl.ds(k0, rk), :]
        vw = v_ref[pl.ds(k0, rk), :]
        out = None
        for hi, msk in enumerate(masks):
            qh = jnp.where(msk, q, jnp.zeros_like(q))
            s_w = _dot_nt(qh, kw) + bias_ref[hi, var]
            s_c = _dot_nt(qh, kc)
            m = _row_reduce(jnp.maximum, jnp.max, s_w, s_c)
            p_w = jnp.exp2(s_w - m)
            p_c = jnp.exp2(s_c - m)
            l = _row_reduce(jnp.add, jnp.sum, p_w, p_c)
            o = (jnp.dot(p_w.astype(BF16), vw, preferred_element_type=F32)
                 + jnp.dot(p_c.astype(BF16), vc, preferred_element_type=F32)) / l
            out = o if out is None else jnp.where(msk, o, out)
        gt = g_ref[pl.ds(q0, rq), :].astype(F32)
        o_ref[pl.ds(q0, rq), :] = (out * _silu(gt)).astype(o_ref.dtype)
        return carry

    lax.fori_loop(0, n_groups, group, 0, unroll=2)


def _na_bias_tables(rpb, rows, win_r, win_c):
    n_heads = rpb.shape[0]
    r_grp = ATTN_ROWS
    wrows = r_grp + win_r
    cols = np.arange(GRID_W)
    cs = np.clip(cols - win_c // 2, 0, GRID_W - win_c)
    col_ok = (cols[None, :] >= cs[:, None]) & (cols[None, :] < cs[:, None] + win_c)
    padded = jnp.pad(rpb.astype(F32) * LOG2E, ((0, 0), (0, 0), (GRID_W - win_c, GRID_W - win_c)))
    toep = jnp.stack([padded[:, :, GRID_W - 1 - c:2 * GRID_W - 1 - c] for c in range(GRID_W)], axis=2)
    toep = jnp.where(col_ok[None, None], toep, MASK_NEG)
    n_dr = 2 * win_r - 1
    toep = jnp.concatenate([toep, jnp.full((n_heads, 1, GRID_W, GRID_W), MASK_NEG, F32)], axis=1)
    tile_idx = np.full((3, r_grp, wrows), n_dr, np.int32)
    for v, r0 in enumerate((0, r_grp, rows - r_grp)):
        ws = int(np.clip(r0 - win_r // 2, 0, rows - wrows))
        for i in range(r_grp):
            r = r0 + i
            rs = int(np.clip(r - win_r // 2, 0, rows - win_r))
            for j in range(wrows):
                rp = ws + j
                if rs <= rp < rs + win_r:
                    tile_idx[v, i, j] = rp - r + win_r - 1
    tab = jnp.take(toep, tile_idx.reshape(-1), axis=1)
    tab = tab.reshape(n_heads, 3, r_grp, wrows, GRID_W, GRID_W).transpose(0, 1, 2, 4, 3, 5)
    return tab.reshape(n_heads, 3, r_grp * GRID_W, wrows * GRID_W)


def _na_attn(z, zc, off_q, off_kc, w_na, dh, bias):
    bsz, s, _ = z.shape
    lc = zc.shape[1]
    npair = w_na // LANES
    rows = s // GRID_W
    win_r = bias.shape[3] // GRID_W - ATTN_ROWS
    assert rows % ATTN_ROWS == 0 and rows >= ATTN_ROWS + win_r and ATTN_ROWS >= win_r // 2
    assert LANES % dh == 0
    hp = LANES // dh
    n_groups = rows // ATTN_ROWS

    def zspec(off):
        return pl.BlockSpec((None, s, LANES), lambda p, b, off=off: (b, 0, off // LANES + p))

    def cspec(off):
        return pl.BlockSpec((None, lc, LANES), lambda p, b, off=off: (b, 0, off // LANES + p))

    return pl.pallas_call(
        functools.partial(_na_attn_kernel, dh=dh, n_groups=n_groups, rows=rows, win_r=win_r),
        grid=(npair, bsz),
        in_specs=[zspec(off_q), zspec(off_q + w_na), zspec(off_q + 2 * w_na), zspec(off_q + 3 * w_na),
                  cspec(off_kc), cspec(off_kc + w_na),
                  pl.BlockSpec((hp,) + bias.shape[1:], lambda p, b: (p, 0, 0, 0))],
        out_specs=pl.BlockSpec((None, s, LANES), lambda p, b: (b, 0, p)),
        out_shape=jax.ShapeDtypeStruct((bsz, s, w_na), BF16),
        compiler_params=_cparams("parallel", "parallel"),
        name="neighbourhood_attention",
    )(z, z, z, z, zc, zc, bias)


LRU_STEPS = 32


LRU_BLOCK_UNROLL = 4


def _lru_conv_block(zx_ref, zp_ref, zn_ref, cw_ref, cb_ref, cols, has_prev, has_next, nb):
    n = zx_ref.shape[0]
    k_taps = cw_ref.shape[0]
    left = (k_taps - 1) // 2
    x = zx_ref[:, cols].astype(F32)
    xp = jnp.where(has_prev, zp_ref[:, cols].astype(F32), 0.0)[2 * nb - left * nb:]
    xn = jnp.where(has_next, zn_ref[:, cols].astype(F32), 0.0)
    xe = jnp.concatenate([xp, x, xn], axis=0)
    cw = cw_ref[:, cols]
    xl = cb_ref[:, cols]
    for k in range(k_taps):
        xl = xl + cw[k:k + 1] * xe[k * nb:k * nb + n]
    return xl


def _lru_coeff_block(xl, j, cols, wg_ref, bg_ref, lam_ref, a_ref, b_ref, blk):
    tg = jnp.tanh(jnp.dot(xl.astype(BF16), wg_ref[j], preferred_element_type=F32) + bg_ref[j])
    half_c = (0.5 * LRU_C) * jax.nn.log_sigmoid(lam_ref[:, cols])
    log_a = half_c + half_c * tg[:, :blk]
    ig = 0.5 + 0.5 * tg[:, blk:]
    a_ref[:, cols] = jnp.exp(log_a)
    th = jnp.tanh(log_a)
    sq = (-2.0 * th) * (1.0 / (1.0 - th))
    root = jnp.where(sq > 0.0, sq * lax.rsqrt(sq), 0.0)
    b_ref[:, cols] = root * ig * xl


def _lru_fwd_kernel(zx_ref, zp_ref, zn_ref, cw_ref, cb_ref, wg_ref, bg_ref, lam_ref, h0_ref,
                    hs_ref, xl_ref, hT_ref, a_ref, b_ref, h_ref, *, nb, n_blk, blk, steps):
    i = pl.program_id(0)
    last = pl.num_programs(0) - 1

    @pl.when(i == 0)
    def _():
        h_ref[...] = h0_ref[...]

    def block(j, carry):
        cols = pl.ds(pl.multiple_of(j * blk, blk), blk)
        xl = _lru_conv_block(zx_ref, zp_ref, zn_ref, cw_ref, cb_ref, cols, i > 0, i < last, nb)
        xl_ref[:, cols] = xl
        _lru_coeff_block(xl, j, cols, wg_ref, bg_ref, lam_ref, a_ref, b_ref, blk)
        return carry

    lax.fori_loop(0, n_blk, block, 0, unroll=min(n_blk, LRU_BLOCK_UNROLL))

    def step(t, h):
        rows = pl.ds(pl.multiple_of(t * nb, nb), nb)
        h = a_ref[rows, :] * h + b_ref[rows, :]
        hs_ref[rows, :] = h
        return h

    h_ref[...] = lax.fori_loop(0, steps, step, h_ref[...])

    @pl.when(i == last)
    def _():
        hT_ref[...] = h_ref[...]


def _lru_bwd_kernel(xl_ref, zg_ref, hs_ref, res_ref, wg_ref, bg_ref, lam_ref,
                    h0_ref, wo_ref, gm_ref, o_ref, hT_ref, a_ref, b_ref, h_ref, *, nb, n_blk, blk, steps):
    i = pl.program_id(0)
    last = pl.num_programs(0) - 1

    @pl.when(i == 0)
    def _():
        h_ref[...] = h0_ref[...]

    def block(j, carry):
        cols = pl.ds(pl.multiple_of(j * blk, blk), blk)
        _lru_coeff_block(xl_ref[:, cols], j, cols, wg_ref, bg_ref, lam_ref, a_ref, b_ref, blk)
        return carry

    lax.fori_loop(0, n_blk, block, 0, unroll=min(n_blk, LRU_BLOCK_UNROLL))

    def step(k, h):
        rows = pl.ds(pl.multiple_of((steps - 1 - k) * nb, nb), nb)
        h = a_ref[rows, :] * h + b_ref[rows, :]
        b_ref[rows, :] = hs_ref[rows, :] + h
        return h

    h_ref[...] = lax.fori_loop(0, steps, step, h_ref[...])

    @pl.when(i == last)
    def _():
        hT_ref[...] = h_ref[...]

    y = (b_ref[...] * _silu(zg_ref[...].astype(F32))).astype(BF16)
    proj = jnp.dot(y, wo_ref[...], preferred_element_type=F32)
    d = proj.shape[1]
    gated = (proj.reshape(steps, nb, d) * gm_ref[...][None]).reshape(steps * nb, d)
    o_ref[...] = res_ref[...] + gated


def _rglru(z, res, nb, w_rnn, conv_w, conv_b, wg, bg, lam, h0, w_out, gate):
    n_rows = z.shape[0]
    d = res.shape[1]
    t = n_rows // nb
    steps = min(t, LRU_STEPS)
    n_tiles = t // steps
    tile = steps * nb
    halo = 2 * nb
    n_halo = n_rows // halo
    per_tile = tile // halo
    n_blk, blk = wg.shape[1], wg.shape[2]
    assert nb % 8 == 0 and conv_w.shape[0] == 4

    kw = dict(nb=nb, n_blk=n_blk, blk=blk, steps=steps)
    const2 = lambda i: (0, 0)
    const3 = lambda i: (0, 0, 0)

    def specs(tile_of):
        return [pl.BlockSpec((tile, w_rnn), lambda i: (tile_of(i), 0)),
                pl.BlockSpec((halo, w_rnn), lambda i: (jnp.maximum(tile_of(i) * per_tile - 1, 0), 0)),
                pl.BlockSpec((halo, w_rnn), lambda i: (jnp.minimum((tile_of(i) + 1) * per_tile, n_halo - 1), 0))]

    conv_specs = [pl.BlockSpec(conv_w.shape, const2), pl.BlockSpec((1, w_rnn), const2)]
    gate_specs = [pl.BlockSpec((n_blk, blk, 2 * blk), const3), pl.BlockSpec((n_blk, 1, 2 * blk), const3),
                  pl.BlockSpec((1, w_rnn), const2), pl.BlockSpec((nb, w_rnn), const2)]
    scratch = [pltpu.VMEM((tile, w_rnn), F32), pltpu.VMEM((tile, w_rnn), F32), pltpu.VMEM((nb, w_rnn), F32)]

    fwd_tile = lambda i: i
    hs, xl, h_fwd = pl.pallas_call(
        functools.partial(_lru_fwd_kernel, **kw),
        grid=(n_tiles,),
        in_specs=specs(fwd_tile) + conv_specs + gate_specs,
        out_specs=[pl.BlockSpec((tile, w_rnn), lambda i: (i, 0)), pl.BlockSpec((tile, w_rnn), lambda i: (i, 0)),
                   pl.BlockSpec((nb, w_rnn), const2)],
        out_shape=[jax.ShapeDtypeStruct((n_rows, w_rnn), F32), jax.ShapeDtypeStruct((n_rows, w_rnn), F32),
                   jax.ShapeDtypeStruct((nb, w_rnn), F32)],
        scratch_shapes=scratch,
        compiler_params=_cparams("arbitrary"),
        name="rglru_forward",
    )(z, z, z, conv_w, conv_b.reshape(1, w_rnn), wg[0], bg[0], lam[0:1], h0[0])

    bwd_tile = lambda i: n_tiles - 1 - i
    out, h_bwd = pl.pallas_call(
        functools.partial(_lru_bwd_kernel, **kw),
        grid=(n_tiles,),
        in_specs=([pl.BlockSpec((tile, w_rnn), lambda i: (bwd_tile(i), 0)),
                   pl.BlockSpec((tile, w_rnn), lambda i: (bwd_tile(i), 1)),
                   pl.BlockSpec((tile, w_rnn), lambda i: (bwd_tile(i), 0)),
                   pl.BlockSpec((tile, d), lambda i: (bwd_tile(i), 0))]
                  + gate_specs
                  + [pl.BlockSpec((w_rnn, d), const2), pl.BlockSpec((nb, d), const2)]),
        out_specs=[pl.BlockSpec((tile, d), lambda i: (bwd_tile(i), 0)), pl.BlockSpec((nb, w_rnn), const2)],
        out_shape=[jax.ShapeDtypeStruct((n_rows, d), F32), jax.ShapeDtypeStruct((nb, w_rnn), F32)],
        scratch_shapes=scratch,
        compiler_params=_cparams("arbitrary"),
        name="rglru_backward_out_proj",
    )(xl, z, hs, res, wg[1], bg[1], lam[1:2], h0[1], w_out, gate)
    return out, jnp.stack([h_fwd, h_bwd])


def _final_norm_kernel(x_ref, g_ref, o_ref):
    nb, tm, d = o_ref.shape
    x = x_ref[...]
    y = x * lax.rsqrt(jnp.mean(x * x, axis=-1, keepdims=True) + EPS) * g_ref[...]
    o_ref[...] = jnp.swapaxes(y.reshape(tm, nb, d), 0, 1)


def _final_norm(h_tb, nb, g):
    d = g.shape[0]
    t = h_tb.shape[0] // nb
    tm = min(t, PROJ_STEPS)
    return pl.pallas_call(
        _final_norm_kernel,
        grid=(t // tm,),
        in_specs=[pl.BlockSpec((tm * nb, d), lambda i: (i, 0)), pl.BlockSpec((1, d), lambda i: (0, 0))],
        out_specs=pl.BlockSpec((nb, tm, d), lambda i: (0, i, 0)),
        out_shape=jax.ShapeDtypeStruct((nb, t, d), F32),
        compiler_params=_cparams("parallel"),
        name="final_rmsnorm",
    )(h_tb, g.reshape(1, d))


def kernel(x, c, ctx, c_ctx, norm_g, w_ada, b_ada, w_in_e, conv_a, rpb, w_out_e, w_in_o, conv_c, conv_c_b,
           w_ra, b_ra, w_ri, b_ri, lru_lam, w_out_o, final_g):
    bsz, s, d = x.shape
    lc = ctx.shape[1]
    depth = norm_g.shape[0]
    w_conv = conv_a.shape[2]
    n_heads = rpb.shape[1]
    win_r, win_c = (rpb.shape[2] + 1) // 2, (rpb.shape[3] + 1) // 2
    w_na = (w_in_e.shape[2] - 4 * w_conv) // 4
    dh = w_na // n_heads
    off_q = 4 * w_conv
    w_rnn = conv_c.shape[2]
    n_blk, blk = w_ra.shape[2], w_ra.shape[3]
    rows = s // GRID_W

    mod = _modulation(c, c_ctx, w_ada, b_ada)

    q_scale = jnp.ones((w_in_e.shape[2],), F32).at[off_q:off_q + w_na].set(dh ** -0.5 * LOG2E)
    w_in_e_bf = (w_in_e * q_scale).astype(BF16)
    w_out_e_bf = w_out_e.astype(BF16)
    w_in_o_bf = w_in_o.astype(BF16)
    w_out_o_bf = w_out_o.astype(BF16)
    n_even, n_odd = w_in_e.shape[0], w_in_o.shape[0]
    bias_all = _na_bias_tables(rpb.reshape((n_even * n_heads,) + rpb.shape[2:]), rows, win_r, win_c)
    bias_all = bias_all.reshape((n_even, n_heads) + bias_all.shape[1:])
    wg_all = (0.5 * jnp.concatenate([w_ra, w_ri], axis=-1)).astype(BF16)
    bg_all = 0.5 * jnp.concatenate([b_ra.reshape(n_odd, 2, n_blk, 1, blk),
                                    b_ri.reshape(n_odd, 2, n_blk, 1, blk)], axis=-1)

    h_lat, lat_layout = x, "bsd"
    h_ctx, ctx_layout = ctx, "bsd"
    for layer in range(depth):
        need_ctx = layer < depth - 1
        m_lat = mod[layer, :bsz]
        m_ctx = jnp.broadcast_to(mod[layer, bsz], (bsz, 3 * d))
        shift, scale, gate = (m_lat[:, k * d:(k + 1) * d] for k in range(3))
        shift_c, scale_c, gate_c = (m_ctx[:, k * d:(k + 1) * d] for k in range(3))
        g = norm_g[layer]
        i = layer // 2
        if layer % 2 == 0:
            w_in = w_in_e_bf[i]
            w_out = w_out_e_bf[i]
            z = _in_proj(h_lat, lat_layout, bsz, g, scale, shift, w_in, "bsd")
            zc = _in_proj(h_ctx, ctx_layout, bsz, g, scale_c, shift_c, w_in, "bsd")
            y_a = _conv_mix(z, conv_a[i], w_conv)
            y_b = _na_attn(z, zc, off_q, off_q + w_na, w_na, dh, bias_all[i])
            h_lat = _out_proj(y_a, y_b, w_out[:w_conv], w_out[w_conv:], h_lat, lat_layout, gate)
            lat_layout = "tb"
            if need_ctx:
                yc_a = _conv_mix(zc, conv_a[i], w_conv)
                yc_b = _ctx_attn(zc, off_q, w_na, dh)
                h_ctx = _out_proj(yc_a, yc_b, w_out[:w_conv], w_out[w_conv:], h_ctx, ctx_layout, gate_c)
                ctx_layout = "tb"
        else:
            assert lat_layout == "tb" and ctx_layout == "tb"
            w_in = w_in_o_bf[i]
            z = _in_proj(h_lat, "tb", bsz, g, scale, shift, w_in, "tb")
            zc = _in_proj(h_ctx, "tb", bsz, g, scale_c, shift_c, w_in, "tb")
            wg, bg, w_out = wg_all[i], bg_all[i], w_out_o_bf[i]
            h0 = jnp.zeros((2, bsz, w_rnn), F32)
            new_ctx, h_ctx_t = _rglru(zc, h_ctx, bsz, w_rnn, conv_c[i], conv_c_b[i],
                                      wg, bg, lru_lam[i], h0, w_out, gate_c)
            h_lat, _ = _rglru(z, h_lat, bsz, w_rnn, conv_c[i], conv_c_b[i],
                              wg, bg, lru_lam[i], h_ctx_t, w_out, gate)
            if need_ctx:
                h_ctx = new_ctx
    assert lat_layout == "tb"
    return _final_norm(h_lat, bsz, final_g)
```

```python
import functools

import numpy as np
import jax
import jax.numpy as jnp
from jax import lax
from jax.experimental import pallas as pl
from jax.experimental.pallas import tpu as pltpu

GRID_W = 64
EPS = 1e-6
LRU_C = 8.0
MASK_NEG = -1e30
LOG2E = 1.4426950408889634
ATTN_ROWS = 4
LANES = 128
VMEM_LIMIT_BYTES = 48 * 1024 * 1024

F32 = jnp.float32
BF16 = jnp.bfloat16


def _cparams(*sem):
    return pltpu.CompilerParams(dimension_semantics=sem, vmem_limit_bytes=VMEM_LIMIT_BYTES)


def _sigmoid(x):
    return 1.0 / (1.0 + jnp.exp(-x))


def _silu(x):
    return x * _sigmoid(x)


def _mod_kernel(s_ref, w_ref, b_ref, o_ref):
    a = _silu(s_ref[...])
    o_ref[...] = jnp.dot(a, w_ref[...], preferred_element_type=F32,
                         precision=lax.Precision.HIGHEST) + b_ref[...]


def _modulation(c, c_ctx, w_ada, b_ada):
    depth, d, n3 = w_ada.shape
    bsz = c.shape[0]
    rows = -(-(bsz + 1) // 8) * 8
    s_in = jnp.zeros((rows, d), F32).at[:bsz].set(c).at[bsz].set(c_ctx)
    tn = d
    return pl.pallas_call(
        _mod_kernel,
        grid=(depth, n3 // tn),
        in_specs=[pl.BlockSpec((rows, d), lambda l, j: (0, 0)),
                  pl.BlockSpec((None, d, tn), lambda l, j: (l, 0, j)),
                  pl.BlockSpec((None, 1, tn), lambda l, j: (l, 0, j))],
        out_specs=pl.BlockSpec((None, rows, tn), lambda l, j: (l, 0, j)),
        out_shape=jax.ShapeDtypeStruct((depth, rows, n3), F32),
        compiler_params=_cparams("parallel", "parallel"),
        name="adaln_mod",
    )(s_in, w_ada, b_ada.reshape(depth, 1, n3))


PROJ_STEPS = 128
PROJ_COLS = 2048


def _in_proj_kernel(x_ref, g_ref, sc_ref, sh_ref, w_ref, o_ref, u_ref, *, nb, tm, in_tb, out_tb):
    d = x_ref.shape[-1]

    @pl.when(pl.program_id(1) == 0)
    def _():
        x = x_ref[...]
        y = x * lax.rsqrt(jnp.mean(x * x, axis=-1, keepdims=True) + EPS) * g_ref[...]
        sc = 1.0 + sc_ref[...]
        sh = sh_ref[...]
        if in_tb:
            u = y.reshape(tm, nb, d) * sc[None] + sh[None]
        else:
            u = y * sc[:, None, :] + sh[:, None, :]
        if in_tb != out_tb:
            u = jnp.swapaxes(u, 0, 1)
        u_ref[...] = u.reshape(tm * nb, d).astype(u_ref.dtype)

    z = jnp.dot(u_ref[...], w_ref[...], preferred_element_type=F32)
    o_ref[...] = z.reshape(o_ref.shape).astype(o_ref.dtype)


def _in_proj(h, h_layout, nb, g, scale, shift, w, out_layout):
    d, n = w.shape
    t = h.shape[1] if h_layout == "bsd" else h.shape[0] // nb
    tm = min(t, PROJ_STEPS)
    tn = min(n, PROJ_COLS)
    if h_layout == "bsd":
        x_spec = pl.BlockSpec((nb, tm, d), lambda i, j: (0, i, 0))
    else:
        x_spec = pl.BlockSpec((tm * nb, d), lambda i, j: (i, 0))
    if out_layout == "bsd":
        o_spec = pl.BlockSpec((nb, tm, tn), lambda i, j: (0, i, j))
        o_shape = jax.ShapeDtypeStruct((nb, t, n), BF16)
    else:
        o_spec = pl.BlockSpec((tm * nb, tn), lambda i, j: (i, j))
        o_shape = jax.ShapeDtypeStruct((t * nb, n), BF16)
    vec_spec = pl.BlockSpec((nb, d), lambda i, j: (0, 0))
    return pl.pallas_call(
        functools.partial(_in_proj_kernel, nb=nb, tm=tm, in_tb=h_layout == "tb", out_tb=out_layout == "tb"),
        grid=(t // tm, n // tn),
        in_specs=[x_spec,
                  pl.BlockSpec((1, d), lambda i, j: (0, 0)),
                  vec_spec, vec_spec,
                  pl.BlockSpec((d, tn), lambda i, j: (0, j))],
        out_specs=o_spec,
        out_shape=o_shape,
        scratch_shapes=[pltpu.VMEM((tm * nb, d), BF16)],
        compiler_params=_cparams("parallel", "arbitrary"),
        name="norm_in_proj",
    )(h, g.reshape(1, d), scale, shift, w)


def _out_proj_kernel(ya_ref, yb_ref, wa_ref, wb_ref, h_ref, gt_ref, o_ref, *, nb, tm, in_tb):
    d = o_ref.shape[-1]
    ya = ya_ref[...].reshape(nb * tm, ya_ref.shape[-1])
    yb = yb_ref[...].reshape(nb * tm, yb_ref.shape[-1])
    proj = jnp.dot(ya, wa_ref[...], preferred_element_type=F32)
    proj += jnp.dot(yb, wb_ref[...], preferred_element_type=F32)
    gated = proj.reshape(nb, tm, d) * gt_ref[...][:, None, :]
    if in_tb:
        o_ref[...] = h_ref[...] + jnp.swapaxes(gated, 0, 1).reshape(tm * nb, d)
    else:
        o_ref[...] = jnp.swapaxes(h_ref[...] + gated, 0, 1).reshape(tm * nb, d)


def _out_proj(ya, yb, wa, wb, h, h_layout, gate):
    nb, t, wa_in = ya.shape
    wb_in = yb.shape[2]
    d = wa.shape[1]
    tm = min(t, PROJ_STEPS)
    if h_layout == "bsd":
        h_spec = pl.BlockSpec((nb, tm, d), lambda i: (0, i, 0))
    else:
        h_spec = pl.BlockSpec((tm * nb, d), lambda i: (i, 0))
    return pl.pallas_call(
        functools.partial(_out_proj_kernel, nb=nb, tm=tm, in_tb=h_layout == "tb"),
        grid=(t // tm,),
        in_specs=[pl.BlockSpec((nb, tm, wa_in), lambda i: (0, i, 0)),
                  pl.BlockSpec((nb, tm, wb_in), lambda i: (0, i, 0)),
                  pl.BlockSpec((wa_in, d), lambda i: (0, 0)),
                  pl.BlockSpec((wb_in, d), lambda i: (0, 0)),
                  h_spec,
                  pl.BlockSpec((nb, d), lambda i: (0, 0))],
        out_specs=pl.BlockSpec((tm * nb, d), lambda i: (i, 0)),
        out_shape=jax.ShapeDtypeStruct((t * nb, d), F32),
        compiler_params=_cparams("parallel"),
        name="out_proj_residual",
    )(ya, yb, wa, wb, h, gate)


CONV_CHUNK = 256
CONV_PAD = 8


def _conv_mix_kernel(zb_ref, zc_ref, zx_ref, zg_ref, w_ref, o_ref, p_ref, *, t, chunk):
    tc = o_ref.shape[1]
    pad = jnp.zeros((CONV_PAD, tc), F32)
    p_ref[0:CONV_PAD, :] = pad
    p_ref[t + CONV_PAD:t + 2 * CONV_PAD, :] = pad
    for s in range(0, t, chunk):
        p_ref[CONV_PAD + s:CONV_PAD + s + chunk, :] = (
            zc_ref[s:s + chunk, :].astype(F32) * zx_ref[s:s + chunk, :].astype(F32))
    w = w_ref[...]
    n_ext = chunk + 2 * CONV_PAD
    for s in range(0, t, chunk):
        ext = p_ref[s:s + n_ext, :]
        prev = pltpu.roll(ext, 1, axis=0)[CONV_PAD:CONV_PAD + chunk]
        nxt = pltpu.roll(ext, n_ext - 1, axis=0)[CONV_PAD:CONV_PAD + chunk]
        cur = ext[CONV_PAD:CONV_PAD + chunk]
        conv = w[0:1] * prev + w[1:2] * cur + w[2:3] * nxt
        y = zb_ref[s:s + chunk, :].astype(F32) * conv * _silu(zg_ref[s:s + chunk, :].astype(F32))
        o_ref[s:s + chunk, :] = y.astype(o_ref.dtype)


def _conv_mix(z, conv_w, w_conv):
    bsz, t, _ = z.shape
    assert conv_w.shape[0] == 3
    tc = min(w_conv, 256)
    nc = w_conv // tc
    chunk = min(t, CONV_CHUNK)

    def zspec(group):
        return pl.BlockSpec((None, t, tc), lambda b, j, group=group: (b, 0, group * nc + j))

    return pl.pallas_call(
        functools.partial(_conv_mix_kernel, t=t, chunk=chunk),
        grid=(bsz, nc),
        in_specs=[zspec(0), zspec(1), zspec(2), zspec(3),
                  pl.BlockSpec((3, tc), lambda b, j: (0, j))],
        out_specs=pl.BlockSpec((None, t, tc), lambda b, j: (b, 0, j)),
        out_shape=jax.ShapeDtypeStruct((bsz, t, w_conv), BF16),
        scratch_shapes=[pltpu.VMEM((t + 2 * CONV_PAD, tc), F32)],
        compiler_params=_cparams("parallel", "parallel"),
        name="gated_conv3",
    )(z, z, z, z, conv_w)


def _head_masks(dh):
    lane = lax.broadcasted_iota(jnp.int32, (1, LANES), 1)
    return [(lane >= i * dh) & (lane < (i + 1) * dh) for i in range(LANES // dh)]


def _row_reduce(op, reduce_fn, *arrays):
    tiles = [x[:, j:j + LANES] for x in arrays if x.shape[1] % LANES == 0 for j in range(0, x.shape[1], LANES)]
    parts = [x for x in arrays if x.shape[1] % LANES != 0]
    while len(tiles) > 1:
        tiles = [op(a, b) for a, b in zip(tiles[0::2], tiles[1::2])] + (tiles[-1:] if len(tiles) % 2 else [])
    out = None
    for x in tiles + parts:
        r = reduce_fn(x, axis=-1, keepdims=True)
        out = r if out is None else op(out, r)
    return out


def _dot_nt(a, b):
    return lax.dot_general(a, b, (((1,), (1,)), ((), ())), preferred_element_type=F32)


def _ctx_attn_kernel(q_ref, k_ref, v_ref, g_ref, o_ref, *, dh):
    masks = _head_masks(dh)
    q = q_ref[...]
    k = k_ref[...]
    v = v_ref[...]
    out = None
    for msk in masks:
        qh = jnp.where(msk, q, jnp.zeros_like(q))
        s = _dot_nt(qh, k)
        p = jnp.exp2(s - jnp.max(s, axis=-1, keepdims=True))
        o = jnp.dot(p.astype(BF16), v, preferred_element_type=F32) / jnp.sum(p, axis=-1, keepdims=True)
        out = o if out is None else jnp.where(msk, o, out)
    o_ref[...] = (out * _silu(g_ref[...].astype(F32))).astype(o_ref.dtype)


def _ctx_attn(zc, off_q, w_na, dh):
    bsz, lc, _ = zc.shape
    npair = w_na // LANES

    def spec(off):
        return pl.BlockSpec((None, lc, LANES), lambda b, p, off=off: (b, 0, off // LANES + p))

    return pl.pallas_call(
        functools.partial(_ctx_attn_kernel, dh=dh),
        grid=(bsz, npair),
        in_specs=[spec(off_q), spec(off_q + w_na), spec(off_q + 2 * w_na), spec(off_q + 3 * w_na)],
        out_specs=pl.BlockSpec((None, lc, LANES), lambda b, p: (b, 0, p)),
        out_shape=jax.ShapeDtypeStruct((bsz, lc, w_na), BF16),
        compiler_params=_cparams("parallel", "parallel"),
        name="context_attention",
    )(zc, zc, zc, zc)


def _build_bias_table(vec_ref, bias_ref, win_c):
    n_h, n_v, r_grp, n_pair = vec_ref.shape[:4]
    qcol = lax.broadcasted_iota(jnp.int32, (GRID_W, LANES), 0)
    lane = lax.broadcasted_iota(jnp.int32, (GRID_W, LANES), 1)
    left_half = lane < GRID_W
    kcol = jnp.where(left_half, lane, lane - GRID_W)
    first = jnp.clip(qcol - win_c // 2, 0, GRID_W - win_c)
    col_ok = (kcol >= first) & (kcol < first + win_c)
    for h in range(n_h):
        for v in range(n_v):
            for i in range(r_grp):
                for m in range(n_pair):
                    skew = [pltpu.roll(jnp.broadcast_to(vec_ref[h, v, i, m, s:s + 1, :], (GRID_W, LANES)),
                                       0, 1, stride=1, stride_axis=0) for s in range(2)]
                    tile = jnp.where(col_ok, jnp.where(left_half, skew[0], skew[1]), MASK_NEG)
                    bias_ref[h, v, i * GRID_W:(i + 1) * GRID_W, m * LANES:(m + 1) * LANES] = tile


def _na_attn_kernel(q_ref, k_ref, v_ref, g_ref, kc_ref, vc_ref, vec_ref, o_ref, bias_ref, *,
                    dh, n_groups, rows, win_r, win_c):
    masks = _head_masks(dh)
    rq = ATTN_ROWS * GRID_W
    wrows = ATTN_ROWS + win_r
    rk = wrows * GRID_W
    kc = kc_ref[...]
    vc = vc_ref[...]

    @pl.when(pl.program_id(1) == 0)
    def _():
        _build_bias_table(vec_ref, bias_ref, win_c)

    def group(g, carry):
        q0 = pl.multiple_of(g * rq, rq)
        ws = jnp.clip(g * ATTN_ROWS - win_r // 2, 0, rows - wrows)
        k0 = pl.multiple_of(ws * GRID_W, GRID_W)
        var = jnp.where(g == 0, 0, jnp.where(g == n_groups - 1, 2, 1))
        q = q_ref[pl.ds(q0, rq), :]
        kw = k_ref[pl.ds(k0, rk), :]
        vw = v_ref[pl.ds(k0, rk), :]
        out = None
        for hi, msk in enumerate(masks):
            qh = jnp.where(msk, q, jnp.zeros_like(q))
            s_w = _dot_nt(qh, kw) + bias_ref[hi, var]
            s_c = _dot_nt(qh, kc)
            m = _row_reduce(jnp.maximum, jnp.max, s_w, s_c)
            p_w = jnp.exp2(s_w - m)
            p_c = jnp.exp2(s_c - m)
            l = _row_reduce(jnp.add, jnp.sum, p_w, p_c)
            o = (jnp.dot(p_w.astype(BF16), vw, preferred_element_type=F32)
                 + jnp.dot(p_c.astype(BF16), vc, preferred_element_type=F32)) / l
            out = o if out is None else jnp.where(msk, o, out)
        gt = g_ref[pl.ds(q0, rq), :].astype(F32)
        o_ref[pl.ds(q0, rq), :] = (out * _silu(gt)).astype(o_ref.dtype)
        return carry

    lax.fori_loop(0, n_groups, group, 0, unroll=2)


def _na_bias_vectors(rpb, rows, win_r, win_c):
    assert 2 * GRID_W == LANES
    n_heads = rpb.shape[0]
    r_grp = ATTN_ROWS
    wrows = r_grp + win_r
    assert wrows % 2 == 0
    n_dr = 2 * win_r - 1
    row_idx = np.full((3, r_grp, wrows), n_dr, np.int32)
    for v, r0 in enumerate((0, r_grp, rows - r_grp)):
        ws = int(np.clip(r0 - win_r // 2, 0, rows - wrows))
        for i in range(r_grp):
            r = r0 + i
            rs = int(np.clip(r - win_r // 2, 0, rows - win_r))
            for j in range(wrows):
                rp = ws + j
                if rs <= rp < rs + win_r:
                    row_idx[v, i, j] = rp - r + win_r - 1
    pad = GRID_W - win_c
    padded = jnp.pad(rpb.astype(F32) * LOG2E, ((0, 0), (0, 1), (pad, pad + 1)), constant_values=MASK_NEG)
    band = jnp.take(padded, row_idx.reshape(-1), axis=1).reshape(n_heads, 3, r_grp, wrows // 2, 2, LANES)
    even = jnp.roll(band[..., 0, :], -(GRID_W - 1), axis=-1)
    odd = jnp.roll(band[..., 1, :], 1, axis=-1)
    return jnp.stack([even, odd], axis=-2)


def _na_attn(z, zc, off_q, off_kc, w_na, dh, bias_vec, win_r, win_c):
    bsz, s, _ = z.shape
    lc = zc.shape[1]
    npair = w_na // LANES
    rows = s // GRID_W
    wrows = ATTN_ROWS + win_r
    assert rows % ATTN_ROWS == 0 and rows >= wrows and ATTN_ROWS >= win_r // 2
    assert LANES % dh == 0
    hp = LANES // dh
    n_groups = rows // ATTN_ROWS

    def zspec(off):
        return pl.BlockSpec((None, s, LANES), lambda p, b, off=off: (b, 0, off // LANES + p))

    def cspec(off):
        return pl.BlockSpec((None, lc, LANES), lambda p, b, off=off: (b, 0, off // LANES + p))

    return pl.pallas_call(
        functools.partial(_na_attn_kernel, dh=dh, n_groups=n_groups, rows=rows, win_r=win_r, win_c=win_c),
        grid=(npair, bsz),
        in_specs=[zspec(off_q), zspec(off_q + w_na), zspec(off_q + 2 * w_na), zspec(off_q + 3 * w_na),
                  cspec(off_kc), cspec(off_kc + w_na),
                  pl.BlockSpec((hp,) + bias_vec.shape[1:], lambda p, b: (p, 0, 0, 0, 0, 0))],
        out_specs=pl.BlockSpec((None, s, LANES), lambda p, b: (b, 0, p)),
        out_shape=jax.ShapeDtypeStruct((bsz, s, w_na), BF16),
        scratch_shapes=[pltpu.VMEM((hp, 3, ATTN_ROWS * GRID_W, wrows * GRID_W), F32)],
        compiler_params=_cparams("arbitrary", "arbitrary"),
        name="neighbourhood_attention",
    )(z, z, z, z, zc, zc, bias_vec)


LRU_STEPS = 32
LRU_BLOCK_UNROLL = 4


def _lru_conv_block(zx_ref, zp_ref, zn_ref, cw_ref, cb_ref, cols, has_prev, has_next, nb):
    n = zx_ref.shape[0]
    k_taps = cw_ref.shape[0]
    left = (k_taps - 1) // 2
    x = zx_ref[:, cols].astype(F32)
    xp = jnp.where(has_prev, zp_ref[:, cols].astype(F32), 0.0)[2 * nb - left * nb:]
    xn = jnp.where(has_next, zn_ref[:, cols].astype(F32), 0.0)
    xe = jnp.concatenate([xp, x, xn], axis=0)
    cw = cw_ref[:, cols]
    xl = cb_ref[:, cols]
    for k in range(k_taps):
        xl = xl + cw[k:k + 1] * xe[k * nb:k * nb + n]
    return xl


def _lru_coeff_block(xl, j, cols, wg_ref, bg_ref, lam_ref, a_ref, b_ref, blk):
    tg = jnp.tanh(jnp.dot(xl.astype(BF16), wg_ref[j], preferred_element_type=F32) + bg_ref[j])
    half_c = (0.5 * LRU_C) * jax.nn.log_sigmoid(lam_ref[:, cols])
    log_a = half_c + half_c * tg[:, :blk]
    ig = 0.5 + 0.5 * tg[:, blk:]
    a_ref[:, cols] = jnp.exp(log_a)
    th = jnp.tanh(log_a)
    sq = (-2.0 * th) * (1.0 / (1.0 - th))
    root = jnp.where(sq > 0.0, sq * lax.rsqrt(sq), 0.0)
    b_ref[:, cols] = root * ig * xl


def _lru_fwd_kernel(zx_ref, zp_ref, zn_ref, cw_ref, cb_ref, wg_ref, bg_ref, lam_ref, h0_ref,
                    hs_ref, xl_ref, hT_ref, a_ref, b_ref, h_ref, *, nb, n_blk, blk, steps):
    i = pl.program_id(0)
    last = pl.num_programs(0) - 1

    @pl.when(i == 0)
    def _():
        h_ref[...] = h0_ref[...]

    def block(j, carry):
        cols = pl.ds(pl.multiple_of(j * blk, blk), blk)
        xl = _lru_conv_block(zx_ref, zp_ref, zn_ref, cw_ref, cb_ref, cols, i > 0, i < last, nb)
        xl_ref[:, cols] = xl
        _lru_coeff_block(xl, j, cols, wg_ref, bg_ref, lam_ref, a_ref, b_ref, blk)
        return carry

    lax.fori_loop(0, n_blk, block, 0, unroll=min(n_blk, LRU_BLOCK_UNROLL))

    def step(t, h):
        rows = pl.ds(pl.multiple_of(t * nb, nb), nb)
        h = a_ref[rows, :] * h + b_ref[rows, :]
        hs_ref[rows, :] = h
        return h

    h_ref[...] = lax.fori_loop(0, steps, step, h_ref[...])

    @pl.when(i == last)
    def _():
        hT_ref[...] = h_ref[...]


def _lru_bwd_kernel(xl_ref, zg_ref, hs_ref, res_ref, wg_ref, bg_ref, lam_ref,
                    h0_ref, wo_ref, gm_ref, fg_ref, o_ref, hT_ref, a_ref, b_ref, h_ref, *,
                    nb, n_blk, blk, steps, final_norm):
    i = pl.program_id(0)
    last = pl.num_programs(0) - 1

    @pl.when(i == 0)
    def _():
        h_ref[...] = h0_ref[...]

    def block(j, carry):
        cols = pl.ds(pl.multiple_of(j * blk, blk), blk)
        _lru_coeff_block(xl_ref[:, cols], j, cols, wg_ref, bg_ref, lam_ref, a_ref, b_ref, blk)
        return carry

    lax.fori_loop(0, n_blk, block, 0, unroll=min(n_blk, LRU_BLOCK_UNROLL))

    def step(k, h):
        rows = pl.ds(pl.multiple_of((steps - 1 - k) * nb, nb), nb)
        h = a_ref[rows, :] * h + b_ref[rows, :]
        b_ref[rows, :] = hs_ref[rows, :] + h
        return h

    h_ref[...] = lax.fori_loop(0, steps, step, h_ref[...])

    @pl.when(i == last)
    def _():
        hT_ref[...] = h_ref[...]

    y = (b_ref[...] * _silu(zg_ref[...].astype(F32))).astype(BF16)
    proj = jnp.dot(y, wo_ref[...], preferred_element_type=F32)
    d = proj.shape[1]
    gated = (proj.reshape(steps, nb, d) * gm_ref[...][None]).reshape(steps * nb, d)
    out = res_ref[...] + gated
    if final_norm:
        out = out * lax.rsqrt(jnp.mean(out * out, axis=-1, keepdims=True) + EPS) * fg_ref[...]
        o_ref[...] = jnp.swapaxes(out.reshape(steps, nb, d), 0, 1)
    else:
        o_ref[...] = out


def _rglru(z, res, nb, w_rnn, conv_w, conv_b, wg, bg, lam, h0, w_out, gate, final_g, final_norm):
    n_rows = z.shape[0]
    d = res.shape[1]
    t = n_rows // nb
    steps = min(t, LRU_STEPS)
    n_tiles = t // steps
    tile = steps * nb
    halo = 2 * nb
    n_halo = n_rows // halo
    per_tile = tile // halo
    n_blk, blk = wg.shape[1], wg.shape[2]
    assert nb % 8 == 0 and conv_w.shape[0] == 4

    kw = dict(nb=nb, n_blk=n_blk, blk=blk, steps=steps)
    const2 = lambda i: (0, 0)
    const3 = lambda i: (0, 0, 0)

    def specs(tile_of):
        return [pl.BlockSpec((tile, w_rnn), lambda i: (tile_of(i), 0)),
                pl.BlockSpec((halo, w_rnn), lambda i: (jnp.maximum(tile_of(i) * per_tile - 1, 0), 0)),
                pl.BlockSpec((halo, w_rnn), lambda i: (jnp.minimum((tile_of(i) + 1) * per_tile, n_halo - 1), 0))]

    conv_specs = [pl.BlockSpec(conv_w.shape, const2), pl.BlockSpec((1, w_rnn), const2)]
    gate_specs = [pl.BlockSpec((n_blk, blk, 2 * blk), const3), pl.BlockSpec((n_blk, 1, 2 * blk), const3),
                  pl.BlockSpec((1, w_rnn), const2), pl.BlockSpec((nb, w_rnn), const2)]
    scratch = [pltpu.VMEM((tile, w_rnn), F32), pltpu.VMEM((tile, w_rnn), F32), pltpu.VMEM((nb, w_rnn), F32)]

    fwd_tile = lambda i: i
    hs, xl, h_fwd = pl.pallas_call(
        functools.partial(_lru_fwd_kernel, **kw),
        grid=(n_tiles,),
        in_specs=specs(fwd_tile) + conv_specs + gate_specs,
        out_specs=[pl.BlockSpec((tile, w_rnn), lambda i: (i, 0)), pl.BlockSpec((tile, w_rnn), lambda i: (i, 0)),
                   pl.BlockSpec((nb, w_rnn), const2)],
        out_shape=[jax.ShapeDtypeStruct((n_rows, w_rnn), F32), jax.ShapeDtypeStruct((n_rows, w_rnn), F32),
                   jax.ShapeDtypeStruct((nb, w_rnn), F32)],
        scratch_shapes=scratch,
        compiler_params=_cparams("arbitrary"),
        name="rglru_forward",
    )(z, z, z, conv_w, conv_b.reshape(1, w_rnn), wg[0], bg[0], lam[0:1], h0[0])

    bwd_tile = lambda i: n_tiles - 1 - i
    if final_norm:
        out_spec = pl.BlockSpec((nb, steps, d), lambda i: (0, bwd_tile(i), 0))
        out_shape = jax.ShapeDtypeStruct((nb, t, d), F32)
    else:
        out_spec = pl.BlockSpec((tile, d), lambda i: (bwd_tile(i), 0))
        out_shape = jax.ShapeDtypeStruct((n_rows, d), F32)
    out, h_bwd = pl.pallas_call(
        functools.partial(_lru_bwd_kernel, final_norm=final_norm, **kw),
        grid=(n_tiles,),
        in_specs=([pl.BlockSpec((tile, w_rnn), lambda i: (bwd_tile(i), 0)),
                   pl.BlockSpec((tile, w_rnn), lambda i: (bwd_tile(i), 1)),
                   pl.BlockSpec((tile, w_rnn), lambda i: (bwd_tile(i), 0)),
                   pl.BlockSpec((tile, d), lambda i: (bwd_tile(i), 0))]
                  + gate_specs
                  + [pl.BlockSpec((w_rnn, d), const2), pl.BlockSpec((nb, d), const2), pl.BlockSpec((1, d), const2)]),
        out_specs=[out_spec, pl.BlockSpec((nb, w_rnn), const2)],
        out_shape=[out_shape, jax.ShapeDtypeStruct((nb, w_rnn), F32)],
        scratch_shapes=scratch,
        compiler_params=_cparams("arbitrary"),
        name="rglru_backward_out_proj",
    )(xl, z, hs, res, wg[1], bg[1], lam[1:2], h0[1], w_out, gate, final_g.reshape(1, d))
    return out, jnp.stack([h_fwd, h_bwd])


def _final_norm_kernel(x_ref, g_ref, o_ref):
    nb, tm, d = o_ref.shape
    x = x_ref[...]
    y = x * lax.rsqrt(jnp.mean(x * x, axis=-1, keepdims=True) + EPS) * g_ref[...]
    o_ref[...] = jnp.swapaxes(y.reshape(tm, nb, d), 0, 1)


def _final_norm(h_tb, nb, g):
    d = g.shape[0]
    t = h_tb.shape[0] // nb
    tm = min(t, PROJ_STEPS)
    return pl.pallas_call(
        _final_norm_kernel,
        grid=(t // tm,),
        in_specs=[pl.BlockSpec((tm * nb, d), lambda i: (i, 0)), pl.BlockSpec((1, d), lambda i: (0, 0))],
        out_specs=pl.BlockSpec((nb, tm, d), lambda i: (0, i, 0)),
        out_shape=jax.ShapeDtypeStruct((nb, t, d), F32),
        compiler_params=_cparams("parallel"),
        name="final_rmsnorm",
    )(h_tb, g.reshape(1, d))


def kernel(x, c, ctx, c_ctx, norm_g, w_ada, b_ada, w_in_e, conv_a, rpb, w_out_e, w_in_o, conv_c, conv_c_b,
           w_ra, b_ra, w_ri, b_ri, lru_lam, w_out_o, final_g):
    bsz, s, d = x.shape
    depth = norm_g.shape[0]
    w_conv = conv_a.shape[2]
    n_heads = rpb.shape[1]
    win_r, win_c = (rpb.shape[2] + 1) // 2, (rpb.shape[3] + 1) // 2
    w_na = (w_in_e.shape[2] - 4 * w_conv) // 4
    dh = w_na // n_heads
    off_q = 4 * w_conv
    w_rnn = conv_c.shape[2]
    n_blk, blk = w_ra.shape[2], w_ra.shape[3]
    rows = s // GRID_W

    mod = _modulation(c, c_ctx, w_ada, b_ada)

    q_scale = jnp.ones((w_in_e.shape[2],), F32).at[off_q:off_q + w_na].set(dh ** -0.5 * LOG2E)
    w_in_e_bf = (w_in_e * q_scale).astype(BF16)
    w_out_e_bf = w_out_e.astype(BF16)
    w_in_o_bf = w_in_o.astype(BF16)
    w_out_o_bf = w_out_o.astype(BF16)
    n_even, n_odd = w_in_e.shape[0], w_in_o.shape[0]
    bias_vec = _na_bias_vectors(rpb.reshape((n_even * n_heads,) + rpb.shape[2:]), rows, win_r, win_c)
    bias_vec = bias_vec.reshape((n_even, n_heads) + bias_vec.shape[1:])
    wg_all = (0.5 * jnp.concatenate([w_ra, w_ri], axis=-1)).astype(BF16)
    bg_all = 0.5 * jnp.concatenate([b_ra.reshape(n_odd, 2, n_blk, 1, blk),
                                    b_ri.reshape(n_odd, 2, n_blk, 1, blk)], axis=-1)

    h_lat, lat_layout = x, "bsd"
    h_ctx, ctx_layout = ctx, "bsd"
    for layer in range(depth):
        need_ctx = layer < depth - 1
        m_lat = mod[layer, :bsz]
        m_ctx = jnp.broadcast_to(mod[layer, bsz], (bsz, 3 * d))
        shift, scale, gate = (m_lat[:, k * d:(k + 1) * d] for k in range(3))
        shift_c, scale_c, gate_c = (m_ctx[:, k * d:(k + 1) * d] for k in range(3))
        g = norm_g[layer]
        i = layer // 2
        if layer % 2 == 0:
            w_in = w_in_e_bf[i]
            w_out = w_out_e_bf[i]
            z = _in_proj(h_lat, lat_layout, bsz, g, scale, shift, w_in, "bsd")
            zc = _in_proj(h_ctx, ctx_layout, bsz, g, scale_c, shift_c, w_in, "bsd")
            y_a = _conv_mix(z, conv_a[i], w_conv)
            y_b = _na_attn(z, zc, off_q, off_q + w_na, w_na, dh, bias_vec[i], win_r, win_c)
            h_lat = _out_proj(y_a, y_b, w_out[:w_conv], w_out[w_conv:], h_lat, lat_layout, gate)
            lat_layout = "tb"
            if need_ctx:
                yc_a = _conv_mix(zc, conv_a[i], w_conv)
                yc_b = _ctx_attn(zc, off_q, w_na, dh)
                h_ctx = _out_proj(yc_a, yc_b, w_out[:w_conv], w_out[w_conv:], h_ctx, ctx_layout, gate_c)
                ctx_layout = "tb"
        else:
            assert lat_layout == "tb" and ctx_layout == "tb"
            w_in = w_in_o_bf[i]
            z = _in_proj(h_lat, "tb", bsz, g, scale, shift, w_in, "tb")
            zc = _in_proj(h_ctx, "tb", bsz, g, scale_c, shift_c, w_in, "tb")
            wg, bg, w_out = wg_all[i], bg_all[i], w_out_o_bf[i]
            h0 = jnp.zeros((2, bsz, w_rnn), F32)
            new_ctx, h_ctx_t = _rglru(zc, h_ctx, bsz, w_rnn, conv_c[i], conv_c_b[i],
                                      wg, bg, lru_lam[i], h0, w_out, gate_c, final_g, False)
            is_last = layer == depth - 1
            h_lat, _ = _rglru(z, h_lat, bsz, w_rnn, conv_c[i], conv_c_b[i],
                              wg, bg, lru_lam[i], h_ctx_t, w_out, gate, final_g, is_last)
            if is_last:
                return h_lat
            if need_ctx:
                h_ctx = new_ctx
    assert lat_layout == "tb"
    return _final_norm(h_lat, bsz, final_g)
```

```python
import functools

import numpy as np
import jax
import jax.numpy as jnp
from jax import lax
from jax.experimental import pallas as pl
from jax.experimental.pallas import tpu as pltpu

GRID_W = 64
EPS = 1e-6
LRU_C = 8.0
MASK_NEG = -1e30
LOG2E = 1.4426950408889634
ATTN_ROWS = 4
LANES = 128
VMEM_LIMIT_BYTES = 48 * 1024 * 1024

F32 = jnp.float32
BF16 = jnp.bfloat16


def _cparams(*sem):
    return pltpu.CompilerParams(dimension_semantics=sem, vmem_limit_bytes=VMEM_LIMIT_BYTES)


def _sigmoid(x):
    return 1.0 / (1.0 + jnp.exp(-x))


def _silu(x):
    return x * _sigmoid(x)


def _mod_kernel(s_ref, w_ref, b_ref, o_ref):
    a = _silu(s_ref[...])
    o_ref[...] = jnp.dot(a, w_ref[...], preferred_element_type=F32,
                         precision=lax.Precision.HIGHEST) + b_ref[...]


def _modulation(c, c_ctx, w_ada, b_ada):
    depth, d, n3 = w_ada.shape
    bsz = c.shape[0]
    rows = -(-(bsz + 1) // 8) * 8
    s_in = jnp.zeros((rows, d), F32).at[:bsz].set(c).at[bsz].set(c_ctx)
    tn = d
    return pl.pallas_call(
        _mod_kernel,
        grid=(depth, n3 // tn),
        in_specs=[pl.BlockSpec((rows, d), lambda l, j: (0, 0)),
                  pl.BlockSpec((None, d, tn), lambda l, j: (l, 0, j)),
                  pl.BlockSpec((None, 1, tn), lambda l, j: (l, 0, j))],
        out_specs=pl.BlockSpec((None, rows, tn), lambda l, j: (l, 0, j)),
        out_shape=jax.ShapeDtypeStruct((depth, rows, n3), F32),
        compiler_params=_cparams("parallel", "parallel"),
        name="adaln_mod",
    )(s_in, w_ada, b_ada.reshape(depth, 1, n3))


PROJ_STEPS = 128
PROJ_COLS = 2048


def _in_proj_kernel(x_ref, g_ref, sc_ref, sh_ref, w_ref, o_ref, u_ref, *, nb, tm, in_tb, out_tb):
    d = x_ref.shape[-1]

    @pl.when(pl.program_id(1) == 0)
    def _():
        x = x_ref[...]
        y = x * lax.rsqrt(jnp.mean(x * x, axis=-1, keepdims=True) + EPS) * g_ref[...]
        sc = 1.0 + sc_ref[...]
        sh = sh_ref[...]
        if in_tb:
            u = y.reshape(tm, nb, d) * sc[None] + sh[None]
        else:
            u = y * sc[:, None, :] + sh[:, None, :]
        if in_tb != out_tb:
            u = jnp.swapaxes(u, 0, 1)
        u_ref[...] = u.reshape(tm * nb, d).astype(u_ref.dtype)

    z = jnp.dot(u_ref[...], w_ref[...], preferred_element_type=F32)
    o_ref[...] = z.reshape(o_ref.shape).astype(o_ref.dtype)


def _in_proj(h, h_layout, nb, g, scale, shift, w, out_layout):
    d, n = w.shape
    t = h.shape[1] if h_layout == "bsd" else h.shape[0] // nb
    tm = min(t, PROJ_STEPS)
    tn = min(n, PROJ_COLS)
    if h_layout == "bsd":
        x_spec = pl.BlockSpec((nb, tm, d), lambda i, j: (0, i, 0))
    else:
        x_spec = pl.BlockSpec((tm * nb, d), lambda i, j: (i, 0))
    if out_layout == "bsd":
        o_spec = pl.BlockSpec((nb, tm, tn), lambda i, j: (0, i, j))
        o_shape = jax.ShapeDtypeStruct((nb, t, n), BF16)
    else:
        o_spec = pl.BlockSpec((tm * nb, tn), lambda i, j: (i, j))
        o_shape = jax.ShapeDtypeStruct((t * nb, n), BF16)
    vec_spec = pl.BlockSpec((nb, d), lambda i, j: (0, 0))
    return pl.pallas_call(
        functools.partial(_in_proj_kernel, nb=nb, tm=tm, in_tb=h_layout == "tb", out_tb=out_layout == "tb"),
        grid=(t // tm, n // tn),
        in_specs=[x_spec,
                  pl.BlockSpec((1, d), lambda i, j: (0, 0)),
                  vec_spec, vec_spec,
                  pl.BlockSpec((d, tn), lambda i, j: (0, j))],
        out_specs=o_spec,
        out_shape=o_shape,
        scratch_shapes=[pltpu.VMEM((tm * nb, d), BF16)],
        compiler_params=_cparams("parallel", "arbitrary"),
        name="norm_in_proj",
    )(h, g.reshape(1, d), scale, shift, w)


def _out_proj_kernel(ya_ref, yb_ref, wa_ref, wb_ref, h_ref, gt_ref, o_ref, *, nb, tm, in_tb):
    d = o_ref.shape[-1]
    ya = ya_ref[...].reshape(nb * tm, ya_ref.shape[-1])
    yb = yb_ref[...].reshape(nb * tm, yb_ref.shape[-1])
    proj = jnp.dot(ya, wa_ref[...], preferred_element_type=F32)
    proj += jnp.dot(yb, wb_ref[...], preferred_element_type=F32)
    gated = proj.reshape(nb, tm, d) * gt_ref[...][:, None, :]
    if in_tb:
        o_ref[...] = h_ref[...] + jnp.swapaxes(gated, 0, 1).reshape(tm * nb, d)
    else:
        o_ref[...] = jnp.swapaxes(h_ref[...] + gated, 0, 1).reshape(tm * nb, d)


def _out_proj(ya, yb, wa, wb, h, h_layout, gate):
    nb, t, wa_in = ya.shape
    wb_in = yb.shape[2]
    d = wa.shape[1]
    tm = min(t, PROJ_STEPS)
    if h_layout == "bsd":
        h_spec = pl.BlockSpec((nb, tm, d), lambda i: (0, i, 0))
    else:
        h_spec = pl.BlockSpec((tm * nb, d), lambda i: (i, 0))
    return pl.pallas_call(
        functools.partial(_out_proj_kernel, nb=nb, tm=tm, in_tb=h_layout == "tb"),
        grid=(t // tm,),
        in_specs=[pl.BlockSpec((nb, tm, wa_in), lambda i: (0, i, 0)),
                  pl.BlockSpec((nb, tm, wb_in), lambda i: (0, i, 0)),
                  pl.BlockSpec((wa_in, d), lambda i: (0, 0)),
                  pl.BlockSpec((wb_in, d), lambda i: (0, 0)),
                  h_spec,
                  pl.BlockSpec((nb, d), lambda i: (0, 0))],
        out_specs=pl.BlockSpec((tm * nb, d), lambda i: (i, 0)),
        out_shape=jax.ShapeDtypeStruct((t * nb, d), F32),
        compiler_params=_cparams("parallel"),
        name="out_proj_residual",
    )(ya, yb, wa, wb, h, gate)


CONV_CHUNK = 256
CONV_PAD = 8


def _conv_mix_kernel(zb_ref, zc_ref, zx_ref, zg_ref, w_ref, o_ref, p_ref, *, t, chunk):
    tc = o_ref.shape[1]
    pad = jnp.zeros((CONV_PAD, tc), F32)
    p_ref[0:CONV_PAD, :] = pad
    p_ref[t + CONV_PAD:t + 2 * CONV_PAD, :] = pad
    for s in range(0, t, chunk):
        p_ref[CONV_PAD + s:CONV_PAD + s + chunk, :] = (
            zc_ref[s:s + chunk, :].astype(F32) * zx_ref[s:s + chunk, :].astype(F32))
    w = w_ref[...]
    n_ext = chunk + 2 * CONV_PAD
    for s in range(0, t, chunk):
        ext = p_ref[s:s + n_ext, :]
        prev = pltpu.roll(ext, 1, axis=0)[CONV_PAD:CONV_PAD + chunk]
        nxt = pltpu.roll(ext, n_ext - 1, axis=0)[CONV_PAD:CONV_PAD + chunk]
        cur = ext[CONV_PAD:CONV_PAD + chunk]
        conv = w[0:1] * prev + w[1:2] * cur + w[2:3] * nxt
        y = zb_ref[s:s + chunk, :].astype(F32) * conv * _silu(zg_ref[s:s + chunk, :].astype(F32))
        o_ref[s:s + chunk, :] = y.astype(o_ref.dtype)


def _conv_mix(z, conv_w, w_conv):
    bsz, t, _ = z.shape
    assert conv_w.shape[0] == 3
    tc = min(w_conv, 256)
    nc = w_conv // tc
    chunk = min(t, CONV_CHUNK)

    def zspec(group):
        return pl.BlockSpec((None, t, tc), lambda b, j, group=group: (b, 0, group * nc + j))

    return pl.pallas_call(
        functools.partial(_conv_mix_kernel, t=t, chunk=chunk),
        grid=(bsz, nc),
        in_specs=[zspec(0), zspec(1), zspec(2), zspec(3),
                  pl.BlockSpec((3, tc), lambda b, j: (0, j))],
        out_specs=pl.BlockSpec((None, t, tc), lambda b, j: (b, 0, j)),
        out_shape=jax.ShapeDtypeStruct((bsz, t, w_conv), BF16),
        scratch_shapes=[pltpu.VMEM((t + 2 * CONV_PAD, tc), F32)],
        compiler_params=_cparams("parallel", "parallel"),
        name="gated_conv3",
    )(z, z, z, z, conv_w)


def _head_masks(dh):
    lane = lax.broadcasted_iota(jnp.int32, (1, LANES), 1)
    return [(lane >= i * dh) & (lane < (i + 1) * dh) for i in range(LANES // dh)]


def _row_reduce(op, reduce_fn, *arrays):
    tiles = [x[:, j:j + LANES] for x in arrays if x.shape[1] % LANES == 0 for j in range(0, x.shape[1], LANES)]
    parts = [x for x in arrays if x.shape[1] % LANES != 0]
    while len(tiles) > 1:
        tiles = [op(a, b) for a, b in zip(tiles[0::2], tiles[1::2])] + (tiles[-1:] if len(tiles) % 2 else [])
    out = None
    for x in tiles + parts:
        r = reduce_fn(x, axis=-1, keepdims=True)
        out = r if out is None else op(out, r)
    return out


def _dot_nt(a, b):
    return lax.dot_general(a, b, (((1,), (1,)), ((), ())), preferred_element_type=F32)


def _ctx_attn_kernel(q_ref, k_ref, v_ref, g_ref, o_ref, *, dh):
    masks = _head_masks(dh)
    for c0 in range(0, q_ref.shape[1], LANES):
        cols = slice(c0, c0 + LANES)
        q = q_ref[:, cols]
        k = k_ref[:, cols]
        v = v_ref[:, cols]
        out = None
        for msk in masks:
            qh = jnp.where(msk, q, jnp.zeros_like(q))
            s = _dot_nt(qh, k)
            p = jnp.exp2(s - jnp.max(s, axis=-1, keepdims=True))
            o = jnp.dot(p.astype(BF16), v, preferred_element_type=F32) / jnp.sum(p, axis=-1, keepdims=True)
            out = o if out is None else jnp.where(msk, o, out)
        o_ref[:, cols] = (out * _silu(g_ref[:, cols].astype(F32))).astype(o_ref.dtype)


def _ctx_attn(zc, off_q, w_na, dh):
    bsz, lc, _ = zc.shape
    assert off_q % w_na == 0 and w_na % LANES == 0

    def spec(off):
        return pl.BlockSpec((None, lc, w_na), lambda b, off=off: (b, 0, off // w_na))

    return pl.pallas_call(
        functools.partial(_ctx_attn_kernel, dh=dh),
        grid=(bsz,),
        in_specs=[spec(off_q), spec(off_q + w_na), spec(off_q + 2 * w_na), spec(off_q + 3 * w_na)],
        out_specs=pl.BlockSpec((None, lc, w_na), lambda b: (b, 0, 0)),
        out_shape=jax.ShapeDtypeStruct((bsz, lc, w_na), BF16),
        compiler_params=_cparams("parallel"),
        name="context_attention",
    )(zc, zc, zc, zc)


def _build_bias_table(vec_ref, bias_ref, win_c):
    n_h, n_v, r_grp, n_pair = vec_ref.shape[:4]
    qcol = lax.broadcasted_iota(jnp.int32, (GRID_W, LANES), 0)
    lane = lax.broadcasted_iota(jnp.int32, (GRID_W, LANES), 1)
    left_half = lane < GRID_W
    kcol = jnp.where(left_half, lane, lane - GRID_W)
    first = jnp.clip(qcol - win_c // 2, 0, GRID_W - win_c)
    col_ok = (kcol >= first) & (kcol < first + win_c)
    for h in range(n_h):
        for v in range(n_v):
            for i in range(r_grp):
                for m in range(n_pair):
                    skew = [pltpu.roll(jnp.broadcast_to(vec_ref[h, v, i, m, s:s + 1, :], (GRID_W, LANES)),
                                       0, 1, stride=1, stride_axis=0) for s in range(2)]
                    tile = jnp.where(col_ok, jnp.where(left_half, skew[0], skew[1]), MASK_NEG)
                    bias_ref[h, v, i * GRID_W:(i + 1) * GRID_W, m * LANES:(m + 1) * LANES] = tile


def _na_attn_kernel(q_ref, k_ref, v_ref, g_ref, kc_ref, vc_ref, vec_ref, o_ref, bias_ref, *,
                    dh, n_groups, rows, win_r, win_c):
    masks = _head_masks(dh)
    rq = ATTN_ROWS * GRID_W
    wrows = ATTN_ROWS + win_r
    rk = wrows * GRID_W
    kc = kc_ref[...]
    vc = vc_ref[...]

    @pl.when(pl.program_id(1) == 0)
    def _():
        _build_bias_table(vec_ref, bias_ref, win_c)

    def group(g, carry):
        q0 = pl.multiple_of(g * rq, rq)
        ws = jnp.clip(g * ATTN_ROWS - win_r // 2, 0, rows - wrows)
        k0 = pl.multiple_of(ws * GRID_W, GRID_W)
        var = jnp.where(g == 0, 0, jnp.where(g == n_groups - 1, 2, 1))
        q = q_ref[pl.ds(q0, rq), :]
        kw = k_ref[pl.ds(k0, rk), :]
        vw = v_ref[pl.ds(k0, rk), :]
        out = None
        for hi, msk in enumerate(masks):
            qh = jnp.where(msk, q, jnp.zeros_like(q))
            s_w = _dot_nt(qh, kw) + bias_ref[hi, var]
            s_c = _dot_nt(qh, kc)
            m = _row_reduce(jnp.maximum, jnp.max, s_w, s_c)
            p_w = jnp.exp2(s_w - m)
            p_c = jnp.exp2(s_c - m)
            l = _row_reduce(jnp.add, jnp.sum, p_w, p_c)
            o = (jnp.dot(p_w.astype(BF16), vw, preferred_element_type=F32)
                 + jnp.dot(p_c.astype(BF16), vc, preferred_element_type=F32)) / l
            out = o if out is None else jnp.where(msk, o, out)
        gt = g_ref[pl.ds(q0, rq), :].astype(F32)
        o_ref[pl.ds(q0, rq), :] = (out * _silu(gt)).astype(o_ref.dtype)
        return carry

    lax.fori_loop(0, n_groups, group, 0, unroll=2)


def _na_bias_vectors(rpb, rows, win_r, win_c):
    assert 2 * GRID_W == LANES
    n_heads = rpb.shape[0]
    r_grp = ATTN_ROWS
    wrows = r_grp + win_r
    assert wrows % 2 == 0
    n_dr = 2 * win_r - 1
    row_idx = np.full((3, r_grp, wrows), n_dr, np.int32)
    for v, r0 in enumerate((0, r_grp, rows - r_grp)):
        ws = int(np.clip(r0 - win_r // 2, 0, rows - wrows))
        for i in range(r_grp):
            r = r0 + i
            rs = int(np.clip(r - win_r // 2, 0, rows - win_r))
            for j in range(wrows):
                rp = ws + j
                if rs <= rp < rs + win_r:
                    row_idx[v, i, j] = rp - r + win_r - 1
    pad = GRID_W - win_c
    padded = jnp.pad(rpb.astype(F32) * LOG2E, ((0, 0), (0, 1), (pad, pad + 1)), constant_values=MASK_NEG)
    band = jnp.take(padded, row_idx.reshape(-1), axis=1).reshape(n_heads, 3, r_grp, wrows // 2, 2, LANES)
    even = jnp.roll(band[..., 0, :], -(GRID_W - 1), axis=-1)
    odd = jnp.roll(band[..., 1, :], 1, axis=-1)
    return jnp.stack([even, odd], axis=-2)


def _na_attn(z, zc, off_q, off_kc, w_na, dh, bias_vec, win_r, win_c):
    bsz, s, _ = z.shape
    lc = zc.shape[1]
    npair = w_na // LANES
    rows = s // GRID_W
    wrows = ATTN_ROWS + win_r
    assert rows % ATTN_ROWS == 0 and rows >= wrows and ATTN_ROWS >= win_r // 2
    assert LANES % dh == 0
    hp = LANES // dh
    n_groups = rows // ATTN_ROWS

    def zspec(off):
        return pl.BlockSpec((None, s, LANES), lambda p, b, off=off: (b, 0, off // LANES + p))

    def cspec(off):
        return pl.BlockSpec((None, lc, LANES), lambda p, b, off=off: (b, 0, off // LANES + p))

    return pl.pallas_call(
        functools.partial(_na_attn_kernel, dh=dh, n_groups=n_groups, rows=rows, win_r=win_r, win_c=win_c),
        grid=(npair, bsz),
        in_specs=[zspec(off_q), zspec(off_q + w_na), zspec(off_q + 2 * w_na), zspec(off_q + 3 * w_na),
                  cspec(off_kc), cspec(off_kc + w_na),
                  pl.BlockSpec((hp,) + bias_vec.shape[1:], lambda p, b: (p, 0, 0, 0, 0, 0))],
        out_specs=pl.BlockSpec((None, s, LANES), lambda p, b: (b, 0, p)),
        out_shape=jax.ShapeDtypeStruct((bsz, s, w_na), BF16),
        scratch_shapes=[pltpu.VMEM((hp, 3, ATTN_ROWS * GRID_W, wrows * GRID_W), F32)],
        compiler_params=_cparams("arbitrary", "arbitrary"),
        name="neighbourhood_attention",
    )(z, z, z, z, zc, zc, bias_vec)


LRU_STEPS = 32
LRU_BLOCK_UNROLL = 4
LRU_SCAN_UNROLL = 4


def _lru_conv_block(zx_ref, zp_ref, zn_ref, cw_ref, cb_ref, cols, has_prev, has_next, nb):
    n = zx_ref.shape[0]
    k_taps = cw_ref.shape[0]
    left = (k_taps - 1) // 2
    x = zx_ref[:, cols].astype(F32)
    xp = jnp.where(has_prev, zp_ref[:, cols].astype(F32), 0.0)[2 * nb - left * nb:]
    xn = jnp.where(has_next, zn_ref[:, cols].astype(F32), 0.0)
    xe = jnp.concatenate([xp, x, xn], axis=0)
    cw = cw_ref[:, cols]
    xl = cb_ref[:, cols]
    for k in range(k_taps):
        xl = xl + cw[k:k + 1] * xe[k * nb:k * nb + n]
    return xl


def _lru_coeff_block(xl, j, cols, wg_ref, bg_ref, lam_ref, a_ref, b_ref, blk):
    tg = jnp.tanh(jnp.dot(xl.astype(BF16), wg_ref[j], preferred_element_type=F32) + bg_ref[j])
    half_c = (0.5 * LRU_C) * jax.nn.log_sigmoid(lam_ref[:, cols])
    log_a = half_c + half_c * tg[:, :blk]
    ig = 0.5 + 0.5 * tg[:, blk:]
    a_ref[:, cols] = jnp.exp(log_a)
    th = jnp.tanh(log_a)
    num = -2.0 * th
    root = jnp.where(num > 0.0, num * lax.rsqrt(num * (1.0 - th)), 0.0)
    b_ref[:, cols] = root * ig * xl


def _lru_fwd_kernel(zx_ref, zp_ref, zn_ref, cw_ref, cb_ref, wg_ref, bg_ref, lam_ref, h0_ref,
                    hs_ref, xl_ref, hT_ref, a_ref, b_ref, h_ref, *, nb, n_blk, blk, steps):
    i = pl.program_id(0)
    last = pl.num_programs(0) - 1

    @pl.when(i == 0)
    def _():
        h_ref[...] = h0_ref[...]

    def block(j, carry):
        cols = pl.ds(pl.multiple_of(j * blk, blk), blk)
        xl = _lru_conv_block(zx_ref, zp_ref, zn_ref, cw_ref, cb_ref, cols, i > 0, i < last, nb)
        xl_ref[:, cols] = xl
        _lru_coeff_block(xl, j, cols, wg_ref, bg_ref, lam_ref, a_ref, b_ref, blk)
        return carry

    lax.fori_loop(0, n_blk, block, 0, unroll=min(n_blk, LRU_BLOCK_UNROLL))

    def step(t, h):
        rows = pl.ds(pl.multiple_of(t * nb, nb), nb)
        h = a_ref[rows, :] * h + b_ref[rows, :]
        hs_ref[rows, :] = h
        return h

    h_ref[...] = lax.fori_loop(0, steps, step, h_ref[...], unroll=min(steps, LRU_SCAN_UNROLL))

    @pl.when(i == last)
    def _():
        hT_ref[...] = h_ref[...]


def _lru_bwd_kernel(xl_ref, zg_ref, hs_ref, res_ref, wg_ref, bg_ref, lam_ref,
                    h0_ref, wo_ref, gm_ref, fg_ref, o_ref, hT_ref, a_ref, b_ref, h_ref, *,
                    nb, n_blk, blk, steps, final_norm):
    i = pl.program_id(0)
    last = pl.num_programs(0) - 1

    @pl.when(i == 0)
    def _():
        h_ref[...] = h0_ref[...]

    def block(j, carry):
        cols = pl.ds(pl.multiple_of(j * blk, blk), blk)
        _lru_coeff_block(xl_ref[:, cols], j, cols, wg_ref, bg_ref, lam_ref, a_ref, b_ref, blk)
        return carry

    lax.fori_loop(0, n_blk, block, 0, unroll=min(n_blk, LRU_BLOCK_UNROLL))

    def step(k, h):
        rows = pl.ds(pl.multiple_of((steps - 1 - k) * nb, nb), nb)
        h = a_ref[rows, :] * h + b_ref[rows, :]
        b_ref[rows, :] = hs_ref[rows, :] + h
        return h

    h_ref[...] = lax.fori_loop(0, steps, step, h_ref[...], unroll=min(steps, LRU_SCAN_UNROLL))

    @pl.when(i == last)
    def _():
        hT_ref[...] = h_ref[...]

    y = (b_ref[...] * _silu(zg_ref[...].astype(F32))).astype(BF16)
    proj = jnp.dot(y, wo_ref[...], preferred_element_type=F32)
    d = proj.shape[1]
    gated = (proj.reshape(steps, nb, d) * gm_ref[...][None]).reshape(steps * nb, d)
    out = res_ref[...] + gated
    if final_norm:
        out = out * lax.rsqrt(jnp.mean(out * out, axis=-1, keepdims=True) + EPS) * fg_ref[...]
        o_ref[...] = jnp.swapaxes(out.reshape(steps, nb, d), 0, 1)
    else:
        o_ref[...] = out


def _rglru(z, res, nb, w_rnn, conv_w, conv_b, wg, bg, lam, h0, w_out, gate, final_g, final_norm):
    n_rows = z.shape[0]
    d = res.shape[1]
    t = n_rows // nb
    steps = min(t, LRU_STEPS)
    n_tiles = t // steps
    tile = steps * nb
    halo = 2 * nb
    n_halo = n_rows // halo
    per_tile = tile // halo
    n_blk, blk = wg.shape[1], wg.shape[2]
    assert nb % 8 == 0 and conv_w.shape[0] == 4

    kw = dict(nb=nb, n_blk=n_blk, blk=blk, steps=steps)
    const2 = lambda i: (0, 0)
    const3 = lambda i: (0, 0, 0)

    def specs(tile_of):
        return [pl.BlockSpec((tile, w_rnn), lambda i: (tile_of(i), 0)),
                pl.BlockSpec((halo, w_rnn), lambda i: (jnp.maximum(tile_of(i) * per_tile - 1, 0), 0)),
                pl.BlockSpec((halo, w_rnn), lambda i: (jnp.minimum((tile_of(i) + 1) * per_tile, n_halo - 1), 0))]

    conv_specs = [pl.BlockSpec(conv_w.shape, const2), pl.BlockSpec((1, w_rnn), const2)]
    gate_specs = [pl.BlockSpec((n_blk, blk, 2 * blk), const3), pl.BlockSpec((n_blk, 1, 2 * blk), const3),
                  pl.BlockSpec((1, w_rnn), const2), pl.BlockSpec((nb, w_rnn), const2)]
    scratch = [pltpu.VMEM((tile, w_rnn), F32), pltpu.VMEM((tile, w_rnn), F32), pltpu.VMEM((nb, w_rnn), F32)]

    fwd_tile = lambda i: i
    hs, xl, h_fwd = pl.pallas_call(
        functools.partial(_lru_fwd_kernel, **kw),
        grid=(n_tiles,),
        in_specs=specs(fwd_tile) + conv_specs + gate_specs,
        out_specs=[pl.BlockSpec((tile, w_rnn), lambda i: (i, 0)), pl.BlockSpec((tile, w_rnn), lambda i: (i, 0)),
                   pl.BlockSpec((nb, w_rnn), const2)],
        out_shape=[jax.ShapeDtypeStruct((n_rows, w_rnn), F32), jax.ShapeDtypeStruct((n_rows, w_rnn), F32),
                   jax.ShapeDtypeStruct((nb, w_rnn), F32)],
        scratch_shapes=scratch,
        compiler_params=_cparams("arbitrary"),
        name="rglru_forward",
    )(z, z, z, conv_w, conv_b.reshape(1, w_rnn), wg[0], bg[0], lam[0:1], h0[0])

    bwd_tile = lambda i: n_tiles - 1 - i
    if final_norm:
        out_spec = pl.BlockSpec((nb, steps, d), lambda i: (0, bwd_tile(i), 0))
        out_shape = jax.ShapeDtypeStruct((nb, t, d), F32)
    else:
        out_spec = pl.BlockSpec((tile, d), lambda i: (bwd_tile(i), 0))
        out_shape = jax.ShapeDtypeStruct((n_rows, d), F32)
    out, h_bwd = pl.pallas_call(
        functools.partial(_lru_bwd_kernel, final_norm=final_norm, **kw),
        grid=(n_tiles,),
        in_specs=([pl.BlockSpec((tile, w_rnn), lambda i: (bwd_tile(i), 0)),
                   pl.BlockSpec((tile, w_rnn), lambda i: (bwd_tile(i), 1)),
                   pl.BlockSpec((tile, w_rnn), lambda i: (bwd_tile(i), 0)),
                   pl.BlockSpec((tile, d), lambda i: (bwd_tile(i), 0))]
                  + gate_specs
                  + [pl.BlockSpec((w_rnn, d), const2), pl.BlockSpec((nb, d), const2), pl.BlockSpec((1, d), const2)]),
        out_specs=[out_spec, pl.BlockSpec((nb, w_rnn), const2)],
        out_shape=[out_shape, jax.ShapeDtypeStruct((nb, w_rnn), F32)],
        scratch_shapes=scratch,
        compiler_params=_cparams("arbitrary"),
        name="rglru_backward_out_proj",
    )(xl, z, hs, res, wg[1], bg[1], lam[1:2], h0[1], w_out, gate, final_g.reshape(1, d))
    return out, jnp.stack([h_fwd, h_bwd])


def _final_norm_kernel(x_ref, g_ref, o_ref):
    nb, tm, d = o_ref.shape
    x = x_ref[...]
    y = x * lax.rsqrt(jnp.mean(x * x, axis=-1, keepdims=True) + EPS) * g_ref[...]
    o_ref[...] = jnp.swapaxes(y.reshape(tm, nb, d), 0, 1)


def _final_norm(h_tb, nb, g):
    d = g.shape[0]
    t = h_tb.shape[0] // nb
    tm = min(t, PROJ_STEPS)
    return pl.pallas_call(
        _final_norm_kernel,
        grid=(t // tm,),
        in_specs=[pl.BlockSpec((tm * nb, d), lambda i: (i, 0)), pl.BlockSpec((1, d), lambda i: (0, 0))],
        out_specs=pl.BlockSpec((nb, tm, d), lambda i: (0, i, 0)),
        out_shape=jax.ShapeDtypeStruct((nb, t, d), F32),
        compiler_params=_cparams("parallel"),
        name="final_rmsnorm",
    )(h_tb, g.reshape(1, d))


def kernel(x, c, ctx, c_ctx, norm_g, w_ada, b_ada, w_in_e, conv_a, rpb, w_out_e, w_in_o, conv_c, conv_c_b,
           w_ra, b_ra, w_ri, b_ri, lru_lam, w_out_o, final_g):
    bsz, s, d = x.shape
    depth = norm_g.shape[0]
    w_conv = conv_a.shape[2]
    n_heads = rpb.shape[1]
    win_r, win_c = (rpb.shape[2] + 1) // 2, (rpb.shape[3] + 1) // 2
    w_na = (w_in_e.shape[2] - 4 * w_conv) // 4
    dh = w_na // n_heads
    off_q = 4 * w_conv
    w_rnn = conv_c.shape[2]
    n_blk, blk = w_ra.shape[2], w_ra.shape[3]
    rows = s // GRID_W

    mod = _modulation(c, c_ctx, w_ada, b_ada)

    q_scale = jnp.ones((w_in_e.shape[2],), F32).at[off_q:off_q + w_na].set(dh ** -0.5 * LOG2E)
    w_in_e_bf = (w_in_e * q_scale).astype(BF16)
    w_out_e_bf = w_out_e.astype(BF16)
    w_in_o_bf = w_in_o.astype(BF16)
    w_out_o_bf = w_out_o.astype(BF16)
    n_even, n_odd = w_in_e.shape[0], w_in_o.shape[0]
    bias_vec = _na_bias_vectors(rpb.reshape((n_even * n_heads,) + rpb.shape[2:]), rows, win_r, win_c)
    bias_vec = bias_vec.reshape((n_even, n_heads) + bias_vec.shape[1:])
    wg_all = (0.5 * jnp.concatenate([w_ra, w_ri], axis=-1)).astype(BF16)
    bg_all = 0.5 * jnp.concatenate([b_ra.reshape(n_odd, 2, n_blk, 1, blk),
                                    b_ri.reshape(n_odd, 2, n_blk, 1, blk)], axis=-1)

    h_lat, lat_layout = x, "bsd"
    h_ctx, ctx_layout = ctx, "bsd"
    for layer in range(depth):
        need_ctx = layer < depth - 1
        m_lat = mod[layer, :bsz]
        m_ctx = jnp.broadcast_to(mod[layer, bsz], (bsz, 3 * d))
        shift, scale, gate = (m_lat[:, k * d:(k + 1) * d] for k in range(3))
        shift_c, scale_c, gate_c = (m_ctx[:, k * d:(k + 1) * d] for k in range(3))
        g = norm_g[layer]
        i = layer // 2
        if layer % 2 == 0:
            w_in = w_in_e_bf[i]
            w_out = w_out_e_bf[i]
            z = _in_proj(h_lat, lat_layout, bsz, g, scale, shift, w_in, "bsd")
            zc = _in_proj(h_ctx, ctx_layout, bsz, g, scale_c, shift_c, w_in, "bsd")
            y_a = _conv_mix(z, conv_a[i], w_conv)
            y_b = _na_attn(z, zc, off_q, off_q + w_na, w_na, dh, bias_vec[i], win_r, win_c)
            h_lat = _out_proj(y_a, y_b, w_out[:w_conv], w_out[w_conv:], h_lat, lat_layout, gate)
            lat_layout = "tb"
            if need_ctx:
                yc_a = _conv_mix(zc, conv_a[i], w_conv)
                yc_b = _ctx_attn(zc, off_q, w_na, dh)
                h_ctx = _out_proj(yc_a, yc_b, w_out[:w_conv], w_out[w_conv:], h_ctx, ctx_layout, gate_c)
                ctx_layout = "tb"
        else:
            assert lat_layout == "tb" and ctx_layout == "tb"
            w_in = w_in_o_bf[i]
            z = _in_proj(h_lat, "tb", bsz, g, scale, shift, w_in, "tb")
            zc = _in_proj(h_ctx, "tb", bsz, g, scale_c, shift_c, w_in, "tb")
            wg, bg, w_out = wg_all[i], bg_all[i], w_out_o_bf[i]
            h0 = jnp.zeros((2, bsz, w_rnn), F32)
            new_ctx, h_ctx_t = _rglru(zc, h_ctx, bsz, w_rnn, conv_c[i], conv_c_b[i],
                                      wg, bg, lru_lam[i], h0, w_out, gate_c, final_g, False)
            is_last = layer == depth - 1
            h_lat, _ = _rglru(z, h_lat, bsz, w_rnn, conv_c[i], conv_c_b[i],
                              wg, bg, lru_lam[i], h_ctx_t, w_out, gate, final_g, is_last)
            if is_last:
                return h_lat
            if need_ctx:
                h_ctx = new_ctx
    assert lat_layout == "tb"
    return _final_norm(h_lat, bsz, final_g)
```

```python
import functools

import numpy as np
import jax
import jax.numpy as jnp
from jax import lax
from jax.experimental import pallas as pl
from jax.experimental.pallas import tpu as pltpu

GRID_W = 64
EPS = 1e-6
LRU_C = 8.0
MASK_NEG = -1e30
LOG2E = 1.4426950408889634
ATTN_ROWS = 4
NA_GROUP_UNROLL = 8
LANES = 128
VMEM_LIMIT_BYTES = 48 * 1024 * 1024

F32 = jnp.float32
BF16 = jnp.bfloat16


def _cparams(*sem):
    return pltpu.CompilerParams(dimension_semantics=sem, vmem_limit_bytes=VMEM_LIMIT_BYTES)


def _sigmoid(x):
    return 1.0 / (1.0 + jnp.exp(-x))


def _silu(x):
    return x * _sigmoid(x)


def _mod_kernel(s_ref, w_ref, b_ref, o_ref):
    a = _silu(s_ref[...])
    o_ref[...] = jnp.dot(a, w_ref[...], preferred_element_type=F32,
                         precision=lax.Precision.HIGHEST) + b_ref[...]


def _modulation(c, c_ctx, w_ada, b_ada):
    depth, d, n3 = w_ada.shape
    bsz = c.shape[0]
    rows = -(-(bsz + 1) // 8) * 8
    s_in = jnp.zeros((rows, d), F32).at[:bsz].set(c).at[bsz].set(c_ctx)
    tn = d
    return pl.pallas_call(
        _mod_kernel,
        grid=(depth, n3 // tn),
        in_specs=[pl.BlockSpec((rows, d), lambda l, j: (0, 0)),
                  pl.BlockSpec((None, d, tn), lambda l, j: (l, 0, j)),
                  pl.BlockSpec((None, 1, tn), lambda l, j: (l, 0, j))],
        out_specs=pl.BlockSpec((None, rows, tn), lambda l, j: (l, 0, j)),
        out_shape=jax.ShapeDtypeStruct((depth, rows, n3), F32),
        compiler_params=_cparams("parallel", "parallel"),
        name="adaln_mod",
    )(s_in, w_ada, b_ada.reshape(depth, 1, n3))


PROJ_STEPS = 128
PROJ_COLS = 2048


def _in_proj_kernel(x_ref, g_ref, sc_ref, sh_ref, w_ref, o_ref, u_ref, *, nb, tm, in_tb, out_tb):
    d = x_ref.shape[-1]

    @pl.when(pl.program_id(1) == 0)
    def _():
        x = x_ref[...]
        y = x * lax.rsqrt(jnp.mean(x * x, axis=-1, keepdims=True) + EPS) * g_ref[...]
        sc = 1.0 + sc_ref[...]
        sh = sh_ref[...]
        if in_tb:
            u = y.reshape(tm, nb, d) * sc[None] + sh[None]
        else:
            u = y * sc[:, None, :] + sh[:, None, :]
        if in_tb != out_tb:
            u = jnp.swapaxes(u, 0, 1)
        u_ref[...] = u.reshape(tm * nb, d).astype(u_ref.dtype)

    z = jnp.dot(u_ref[...], w_ref[...], preferred_element_type=F32)
    o_ref[...] = z.reshape(o_ref.shape).astype(o_ref.dtype)


def _in_proj(h, h_layout, nb, g, scale, shift, w, out_layout):
    d, n = w.shape
    t = h.shape[1] if h_layout == "bsd" else h.shape[0] // nb
    tm = min(t, PROJ_STEPS)
    tn = min(n, PROJ_COLS)
    if h_layout == "bsd":
        x_spec = pl.BlockSpec((nb, tm, d), lambda i, j: (0, i, 0))
    else:
        x_spec = pl.BlockSpec((tm * nb, d), lambda i, j: (i, 0))
    if out_layout == "bsd":
        o_spec = pl.BlockSpec((nb, tm, tn), lambda i, j: (0, i, j))
        o_shape = jax.ShapeDtypeStruct((nb, t, n), BF16)
    else:
        o_spec = pl.BlockSpec((tm * nb, tn), lambda i, j: (i, j))
        o_shape = jax.ShapeDtypeStruct((t * nb, n), BF16)
    vec_spec = pl.BlockSpec((nb, d), lambda i, j: (0, 0))
    return pl.pallas_call(
        functools.partial(_in_proj_kernel, nb=nb, tm=tm, in_tb=h_layout == "tb", out_tb=out_layout == "tb"),
        grid=(t // tm, n // tn),
        in_specs=[x_spec,
                  pl.BlockSpec((1, d), lambda i, j: (0, 0)),
                  vec_spec, vec_spec,
                  pl.BlockSpec((d, tn), lambda i, j: (0, j))],
        out_specs=o_spec,
        out_shape=o_shape,
        scratch_shapes=[pltpu.VMEM((tm * nb, d), BF16)],
        compiler_params=_cparams("parallel", "arbitrary"),
        name="norm_in_proj",
    )(h, g.reshape(1, d), scale, shift, w)


def _out_proj_kernel(ya_ref, yb_ref, wa_ref, wb_ref, h_ref, gt_ref, o_ref, *, nb, tm, in_tb):
    d = o_ref.shape[-1]
    ya = ya_ref[...].reshape(nb * tm, ya_ref.shape[-1])
    yb = yb_ref[...].reshape(nb * tm, yb_ref.shape[-1])
    proj = jnp.dot(ya, wa_ref[...], preferred_element_type=F32)
    proj += jnp.dot(yb, wb_ref[...], preferred_element_type=F32)
    gated = proj.reshape(nb, tm, d) * gt_ref[...][:, None, :]
    if in_tb:
        o_ref[...] = h_ref[...] + jnp.swapaxes(gated, 0, 1).reshape(tm * nb, d)
    else:
        o_ref[...] = jnp.swapaxes(h_ref[...] + gated, 0, 1).reshape(tm * nb, d)


def _out_proj(ya, yb, wa, wb, h, h_layout, gate):
    nb, t, wa_in = ya.shape
    wb_in = yb.shape[2]
    d = wa.shape[1]
    tm = min(t, PROJ_STEPS)
    if h_layout == "bsd":
        h_spec = pl.BlockSpec((nb, tm, d), lambda i: (0, i, 0))
    else:
        h_spec = pl.BlockSpec((tm * nb, d), lambda i: (i, 0))
    return pl.pallas_call(
        functools.partial(_out_proj_kernel, nb=nb, tm=tm, in_tb=h_layout == "tb"),
        grid=(t // tm,),
        in_specs=[pl.BlockSpec((nb, tm, wa_in), lambda i: (0, i, 0)),
                  pl.BlockSpec((nb, tm, wb_in), lambda i: (0, i, 0)),
                  pl.BlockSpec((wa_in, d), lambda i: (0, 0)),
                  pl.BlockSpec((wb_in, d), lambda i: (0, 0)),
                  h_spec,
                  pl.BlockSpec((nb, d), lambda i: (0, 0))],
        out_specs=pl.BlockSpec((tm * nb, d), lambda i: (i, 0)),
        out_shape=jax.ShapeDtypeStruct((t * nb, d), F32),
        compiler_params=_cparams("parallel"),
        name="out_proj_residual",
    )(ya, yb, wa, wb, h, gate)


CONV_CHUNK = 256
CONV_PAD = 8


def _conv_mix_kernel(zb_ref, zc_ref, zx_ref, zg_ref, w_ref, o_ref, p_ref, *, t, chunk):
    tc = o_ref.shape[1]
    pad = jnp.zeros((CONV_PAD, tc), F32)
    p_ref[0:CONV_PAD, :] = pad
    p_ref[t + CONV_PAD:t + 2 * CONV_PAD, :] = pad
    for s in range(0, t, chunk):
        p_ref[CONV_PAD + s:CONV_PAD + s + chunk, :] = (
            zc_ref[s:s + chunk, :].astype(F32) * zx_ref[s:s + chunk, :].astype(F32))
    w = w_ref[...]
    n_ext = chunk + 2 * CONV_PAD
    for s in range(0, t, chunk):
        ext = p_ref[s:s + n_ext, :]
        prev = pltpu.roll(ext, 1, axis=0)[CONV_PAD:CONV_PAD + chunk]
        nxt = pltpu.roll(ext, n_ext - 1, axis=0)[CONV_PAD:CONV_PAD + chunk]
        cur = ext[CONV_PAD:CONV_PAD + chunk]
        conv = w[0:1] * prev + w[1:2] * cur + w[2:3] * nxt
        y = zb_ref[s:s + chunk, :].astype(F32) * conv * _silu(zg_ref[s:s + chunk, :].astype(F32))
        o_ref[s:s + chunk, :] = y.astype(o_ref.dtype)


def _conv_mix(z, conv_w, w_conv):
    bsz, t, _ = z.shape
    assert conv_w.shape[0] == 3
    tc = min(w_conv, 256)
    nc = w_conv // tc
    chunk = min(t, CONV_CHUNK)

    def zspec(group):
        return pl.BlockSpec((None, t, tc), lambda b, j, group=group: (b, 0, group * nc + j))

    return pl.pallas_call(
        functools.partial(_conv_mix_kernel, t=t, chunk=chunk),
        grid=(bsz, nc),
        in_specs=[zspec(0), zspec(1), zspec(2), zspec(3),
                  pl.BlockSpec((3, tc), lambda b, j: (0, j))],
        out_specs=pl.BlockSpec((None, t, tc), lambda b, j: (b, 0, j)),
        out_shape=jax.ShapeDtypeStruct((bsz, t, w_conv), BF16),
        scratch_shapes=[pltpu.VMEM((t + 2 * CONV_PAD, tc), F32)],
        compiler_params=_cparams("parallel", "parallel"),
        name="gated_conv3",
    )(z, z, z, z, conv_w)


def _head_masks(dh):
    lane = lax.broadcasted_iota(jnp.int32, (1, LANES), 1)
    return [(lane >= i * dh) & (lane < (i + 1) * dh) for i in range(LANES // dh)]


def _row_reduce(op, reduce_fn, *arrays):
    tiles = [x[:, j:j + LANES] for x in arrays if x.shape[1] % LANES == 0 for j in range(0, x.shape[1], LANES)]
    parts = [x for x in arrays if x.shape[1] % LANES != 0]
    while len(tiles) > 1:
        tiles = [op(a, b) for a, b in zip(tiles[0::2], tiles[1::2])] + (tiles[-1:] if len(tiles) % 2 else [])
    out = None
    for x in tiles + parts:
        r = reduce_fn(x, axis=-1, keepdims=True)
        out = r if out is None else op(out, r)
    return out


def _dot_nt(a, b):
    return lax.dot_general(a, b, (((1,), (1,)), ((), ())), preferred_element_type=F32)


def _ctx_attn_kernel(q_ref, k_ref, v_ref, g_ref, o_ref, *, dh):
    masks = _head_masks(dh)
    for c0 in range(0, q_ref.shape[1], LANES):
        cols = slice(c0, c0 + LANES)
        q = q_ref[:, cols]
        k = k_ref[:, cols]
        v = v_ref[:, cols]
        out = None
        for msk in masks:
            qh = jnp.where(msk, q, jnp.zeros_like(q))
            s = _dot_nt(qh, k)
            p = jnp.exp2(s - jnp.max(s, axis=-1, keepdims=True))
            o = jnp.dot(p.astype(BF16), v, preferred_element_type=F32) / jnp.sum(p, axis=-1, keepdims=True)
            out = o if out is None else jnp.where(msk, o, out)
        o_ref[:, cols] = (out * _silu(g_ref[:, cols].astype(F32))).astype(o_ref.dtype)


def _ctx_attn(zc, off_q, w_na, dh):
    bsz, lc, _ = zc.shape
    assert off_q % w_na == 0 and w_na % LANES == 0

    def spec(off):
        return pl.BlockSpec((None, lc, w_na), lambda b, off=off: (b, 0, off // w_na))

    return pl.pallas_call(
        functools.partial(_ctx_attn_kernel, dh=dh),
        grid=(bsz,),
        in_specs=[spec(off_q), spec(off_q + w_na), spec(off_q + 2 * w_na), spec(off_q + 3 * w_na)],
        out_specs=pl.BlockSpec((None, lc, w_na), lambda b: (b, 0, 0)),
        out_shape=jax.ShapeDtypeStruct((bsz, lc, w_na), BF16),
        compiler_params=_cparams("parallel"),
        name="context_attention",
    )(zc, zc, zc, zc)


def _build_bias_table(vec_ref, bias_ref, win_c):
    n_h, n_v, r_grp, n_pair = vec_ref.shape[:4]
    qcol = lax.broadcasted_iota(jnp.int32, (GRID_W, LANES), 0)
    lane = lax.broadcasted_iota(jnp.int32, (GRID_W, LANES), 1)
    left_half = lane < GRID_W
    kcol = jnp.where(left_half, lane, lane - GRID_W)
    first = jnp.clip(qcol - win_c // 2, 0, GRID_W - win_c)
    col_ok = (kcol >= first) & (kcol < first + win_c)
    for h in range(n_h):
        for v in range(n_v):
            for i in range(r_grp):
                for m in range(n_pair):
                    skew = [pltpu.roll(jnp.broadcast_to(vec_ref[h, v, i, m, s:s + 1, :], (GRID_W, LANES)),
                                       0, 1, stride=1, stride_axis=0) for s in range(2)]
                    tile = jnp.where(col_ok, jnp.where(left_half, skew[0], skew[1]), MASK_NEG)
                    bias_ref[h, v, i * GRID_W:(i + 1) * GRID_W, m * LANES:(m + 1) * LANES] = tile


def _na_attn_kernel(q_ref, k_ref, v_ref, g_ref, kc_ref, vc_ref, vec_ref, o_ref, bias_ref, *,
                    dh, n_groups, rows, win_r, win_c):
    masks = _head_masks(dh)
    rq = ATTN_ROWS * GRID_W
    wrows = ATTN_ROWS + win_r
    rk = wrows * GRID_W
    kc = kc_ref[...]
    vc = vc_ref[...]

    @pl.when(pl.program_id(1) == 0)
    def _():
        _build_bias_table(vec_ref, bias_ref, win_c)

    def group(g, carry):
        q0 = pl.multiple_of(g * rq, rq)
        ws = jnp.clip(g * ATTN_ROWS - win_r // 2, 0, rows - wrows)
        k0 = pl.multiple_of(ws * GRID_W, GRID_W)
        var = jnp.where(g == 0, 0, jnp.where(g == n_groups - 1, 2, 1))
        q = q_ref[pl.ds(q0, rq), :]
        kw = k_ref[pl.ds(k0, rk), :]
        vw = v_ref[pl.ds(k0, rk), :]
        out = None
        for hi, msk in enumerate(masks):
            qh = jnp.where(msk, q, jnp.zeros_like(q))
            s_w = _dot_nt(qh, kw) + bias_ref[hi, var]
            s_c = _dot_nt(qh, kc)
            m = _row_reduce(jnp.maximum, jnp.max, s_w, s_c)
            p_w = jnp.exp2(s_w - m)
            p_c = jnp.exp2(s_c - m)
            l = _row_reduce(jnp.add, jnp.sum, p_w, p_c)
            o = (jnp.dot(p_w.astype(BF16), vw, preferred_element_type=F32)
                 + jnp.dot(p_c.astype(BF16), vc, preferred_element_type=F32)) / l
            out = o if out is None else jnp.where(msk, o, out)
        gt = g_ref[pl.ds(q0, rq), :].astype(F32)
        o_ref[pl.ds(q0, rq), :] = (out * _silu(gt)).astype(o_ref.dtype)
        return carry

    lax.fori_loop(0, n_groups, group, 0, unroll=min(n_groups, NA_GROUP_UNROLL))


def _na_bias_vectors(rpb, rows, win_r, win_c):
    assert 2 * GRID_W == LANES
    n_heads = rpb.shape[0]
    r_grp = ATTN_ROWS
    wrows = r_grp + win_r
    assert wrows % 2 == 0
    n_dr = 2 * win_r - 1
    row_idx = np.full((3, r_grp, wrows), n_dr, np.int32)
    for v, r0 in enumerate((0, r_grp, rows - r_grp)):
        ws = int(np.clip(r0 - win_r // 2, 0, rows - wrows))
        for i in range(r_grp):
            r = r0 + i
            rs = int(np.clip(r - win_r // 2, 0, rows - win_r))
            for j in range(wrows):
                rp = ws + j
                if rs <= rp < rs + win_r:
                    row_idx[v, i, j] = rp - r + win_r - 1
    pad = GRID_W - win_c
    padded = jnp.pad(rpb.astype(F32) * LOG2E, ((0, 0), (0, 1), (pad, pad + 1)), constant_values=MASK_NEG)
    band = jnp.take(padded, row_idx.reshape(-1), axis=1).reshape(n_heads, 3, r_grp, wrows // 2, 2, LANES)
    even = jnp.roll(band[..., 0, :], -(GRID_W - 1), axis=-1)
    odd = jnp.roll(band[..., 1, :], 1, axis=-1)
    return jnp.stack([even, odd], axis=-2)


def _na_attn(z, zc, off_q, off_kc, w_na, dh, bias_vec, win_r, win_c):
    bsz, s, _ = z.shape
    lc = zc.shape[1]
    npair = w_na // LANES
    rows = s // GRID_W
    wrows = ATTN_ROWS + win_r
    assert rows % ATTN_ROWS == 0 and rows >= wrows and ATTN_ROWS >= win_r // 2
    assert LANES % dh == 0
    hp = LANES // dh
    n_groups = rows // ATTN_ROWS

    def zspec(off):
        return pl.BlockSpec((None, s, LANES), lambda p, b, off=off: (b, 0, off // LANES + p))

    def cspec(off):
        return pl.BlockSpec((None, lc, LANES), lambda p, b, off=off: (b, 0, off // LANES + p))

    return pl.pallas_call(
        functools.partial(_na_attn_kernel, dh=dh, n_groups=n_groups, rows=rows, win_r=win_r, win_c=win_c),
        grid=(npair, bsz),
        in_specs=[zspec(off_q), zspec(off_q + w_na), zspec(off_q + 2 * w_na), zspec(off_q + 3 * w_na),
                  cspec(off_kc), cspec(off_kc + w_na),
                  pl.BlockSpec((hp,) + bias_vec.shape[1:], lambda p, b: (p, 0, 0, 0, 0, 0))],
        out_specs=pl.BlockSpec((None, s, LANES), lambda p, b: (b, 0, p)),
        out_shape=jax.ShapeDtypeStruct((bsz, s, w_na), BF16),
        scratch_shapes=[pltpu.VMEM((hp, 3, ATTN_ROWS * GRID_W, wrows * GRID_W), F32)],
        compiler_params=_cparams("arbitrary", "arbitrary"),
        name="neighbourhood_attention",
    )(z, z, z, z, zc, zc, bias_vec)


LRU_STEPS = 32
LRU_BLOCK_UNROLL = 4
LRU_SCAN_UNROLL = 4


def _lru_conv_block(zx_ref, zp_ref, zn_ref, cw_ref, cb_ref, cols, has_prev, has_next, nb):
    n = zx_ref.shape[0]
    k_taps = cw_ref.shape[0]
    left = (k_taps - 1) // 2
    x = zx_ref[:, cols].astype(F32)
    xp = jnp.where(has_prev, zp_ref[:, cols].astype(F32), 0.0)[2 * nb - left * nb:]
    xn = jnp.where(has_next, zn_ref[:, cols].astype(F32), 0.0)
    xe = jnp.concatenate([xp, x, xn], axis=0)
    cw = cw_ref[:, cols]
    xl = cb_ref[:, cols]
    for k in range(k_taps):
        xl = xl + cw[k:k + 1] * xe[k * nb:k * nb + n]
    return xl


def _lru_coeff_block(xl, j, cols, wg_ref, bg_ref, lam_ref, a_ref, b_ref, blk):
    tg = jnp.tanh(jnp.dot(xl.astype(BF16), wg_ref[j], preferred_element_type=F32) + bg_ref[j])
    half_c = (0.5 * LRU_C) * jax.nn.log_sigmoid(lam_ref[:, cols])
    log_a = half_c + half_c * tg[:, :blk]
    ig2 = 1.0 + tg[:, blk:]
    a_ref[:, cols] = jnp.exp(log_a)
    th = jnp.tanh(log_a)
    num = -0.5 * th
    half_root = jnp.where(num > 0.0, num * lax.rsqrt(num * (1.0 - th)), 0.0)
    b_ref[:, cols] = half_root * ig2 * xl


def _lru_fwd_kernel(zx_ref, zp_ref, zn_ref, cw_ref, cb_ref, wg_ref, bg_ref, lam_ref, h0_ref,
                    hs_ref, xl_ref, hT_ref, a_ref, b_ref, h_ref, *, nb, n_blk, blk, steps):
    i = pl.program_id(0)
    last = pl.num_programs(0) - 1

    @pl.when(i == 0)
    def _():
        h_ref[...] = h0_ref[...]

    def block(j, carry):
        cols = pl.ds(pl.multiple_of(j * blk, blk), blk)
        xl = _lru_conv_block(zx_ref, zp_ref, zn_ref, cw_ref, cb_ref, cols, i > 0, i < last, nb)
        xl_ref[:, cols] = xl
        _lru_coeff_block(xl, j, cols, wg_ref, bg_ref, lam_ref, a_ref, b_ref, blk)
        return carry

    lax.fori_loop(0, n_blk, block, 0, unroll=min(n_blk, LRU_BLOCK_UNROLL))

    def step(t, h):
        rows = pl.ds(pl.multiple_of(t * nb, nb), nb)
        h = a_ref[rows, :] * h + b_ref[rows, :]
        hs_ref[rows, :] = h
        return h

    h_ref[...] = lax.fori_loop(0, steps, step, h_ref[...], unroll=min(steps, LRU_SCAN_UNROLL))

    @pl.when(i == last)
    def _():
        hT_ref[...] = h_ref[...]


def _lru_bwd_kernel(xl_ref, zg_ref, hs_ref, res_ref, wg_ref, bg_ref, lam_ref,
                    h0_ref, wo_ref, gm_ref, fg_ref, o_ref, hT_ref, a_ref, b_ref, h_ref, *,
                    nb, n_blk, blk, steps, final_norm):
    i = pl.program_id(0)
    last = pl.num_programs(0) - 1

    @pl.when(i == 0)
    def _():
        h_ref[...] = h0_ref[...]

    def block(j, carry):
        cols = pl.ds(pl.multiple_of(j * blk, blk), blk)
        _lru_coeff_block(xl_ref[:, cols], j, cols, wg_ref, bg_ref, lam_ref, a_ref, b_ref, blk)
        return carry

    lax.fori_loop(0, n_blk, block, 0, unroll=min(n_blk, LRU_BLOCK_UNROLL))

    def step(k, h):
        rows = pl.ds(pl.multiple_of((steps - 1 - k) * nb, nb), nb)
        h = a_ref[rows, :] * h + b_ref[rows, :]
        b_ref[rows, :] = hs_ref[rows, :] + h
        return h

    h_ref[...] = lax.fori_loop(0, steps, step, h_ref[...], unroll=min(steps, LRU_SCAN_UNROLL))

    @pl.when(i == last)
    def _():
        hT_ref[...] = h_ref[...]

    y = (b_ref[...] * _silu(zg_ref[...].astype(F32))).astype(BF16)
    proj = jnp.dot(y, wo_ref[...], preferred_element_type=F32)
    d = proj.shape[1]
    gated = (proj.reshape(steps, nb, d) * gm_ref[...][None]).reshape(steps * nb, d)
    out = res_ref[...] + gated
    if final_norm:
        out = out * lax.rsqrt(jnp.mean(out * out, axis=-1, keepdims=True) + EPS) * fg_ref[...]
        o_ref[...] = jnp.swapaxes(out.reshape(steps, nb, d), 0, 1)
    else:
        o_ref[...] = out


def _rglru(z, res, nb, w_rnn, conv_w, conv_b, wg, bg, lam, h0, w_out, gate, final_g, final_norm):
    n_rows = z.shape[0]
    d = res.shape[1]
    t = n_rows // nb
    steps = min(t, LRU_STEPS)
    n_tiles = t // steps
    tile = steps * nb
    halo = 2 * nb
    n_halo = n_rows // halo
    per_tile = tile // halo
    n_blk, blk = wg.shape[1], wg.shape[2]
    assert nb % 8 == 0 and conv_w.shape[0] == 4

    kw = dict(nb=nb, n_blk=n_blk, blk=blk, steps=steps)
    const2 = lambda i: (0, 0)
    const3 = lambda i: (0, 0, 0)

    def specs(tile_of):
        return [pl.BlockSpec((tile, w_rnn), lambda i: (tile_of(i), 0)),
                pl.BlockSpec((halo, w_rnn), lambda i: (jnp.maximum(tile_of(i) * per_tile - 1, 0), 0)),
                pl.BlockSpec((halo, w_rnn), lambda i: (jnp.minimum((tile_of(i) + 1) * per_tile, n_halo - 1), 0))]

    conv_specs = [pl.BlockSpec(conv_w.shape, const2), pl.BlockSpec((1, w_rnn), const2)]
    gate_specs = [pl.BlockSpec((n_blk, blk, 2 * blk), const3), pl.BlockSpec((n_blk, 1, 2 * blk), const3),
                  pl.BlockSpec((1, w_rnn), const2), pl.BlockSpec((nb, w_rnn), const2)]
    scratch = [pltpu.VMEM((tile, w_rnn), F32), pltpu.VMEM((tile, w_rnn), F32), pltpu.VMEM((nb, w_rnn), F32)]

    fwd_tile = lambda i: i
    hs, xl, h_fwd = pl.pallas_call(
        functools.partial(_lru_fwd_kernel, **kw),
        grid=(n_tiles,),
        in_specs=specs(fwd_tile) + conv_specs + gate_specs,
        out_specs=[pl.BlockSpec((tile, w_rnn), lambda i: (i, 0)), pl.BlockSpec((tile, w_rnn), lambda i: (i, 0)),
                   pl.BlockSpec((nb, w_rnn), const2)],
        out_shape=[jax.ShapeDtypeStruct((n_rows, w_rnn), F32), jax.ShapeDtypeStruct((n_rows, w_rnn), F32),
                   jax.ShapeDtypeStruct((nb, w_rnn), F32)],
        scratch_shapes=scratch,
        compiler_params=_cparams("arbitrary"),
        name="rglru_forward",
    )(z, z, z, conv_w, conv_b.reshape(1, w_rnn), wg[0], bg[0], lam[0:1], h0[0])

    bwd_tile = lambda i: n_tiles - 1 - i
    if final_norm:
        out_spec = pl.BlockSpec((nb, steps, d), lambda i: (0, bwd_tile(i), 0))
        out_shape = jax.ShapeDtypeStruct((nb, t, d), F32)
    else:
        out_spec = pl.BlockSpec((tile, d), lambda i: (bwd_tile(i), 0))
        out_shape = jax.ShapeDtypeStruct((n_rows, d), F32)
    out, h_bwd = pl.pallas_call(
        functools.partial(_lru_bwd_kernel, final_norm=final_norm, **kw),
        grid=(n_tiles,),
        in_specs=([pl.BlockSpec((tile, w_rnn), lambda i: (bwd_tile(i), 0)),
                   pl.BlockSpec((tile, w_rnn), lambda i: (bwd_tile(i), 1)),
                   pl.BlockSpec((tile, w_rnn), lambda i: (bwd_tile(i), 0)),
                   pl.BlockSpec((tile, d), lambda i: (bwd_tile(i), 0))]
                  + gate_specs
                  + [pl.BlockSpec((w_rnn, d), const2), pl.BlockSpec((nb, d), const2), pl.BlockSpec((1, d), const2)]),
        out_specs=[out_spec, pl.BlockSpec((nb, w_rnn), const2)],
        out_shape=[out_shape, jax.ShapeDtypeStruct((nb, w_rnn), F32)],
        scratch_shapes=scratch,
        compiler_params=_cparams("arbitrary"),
        name="rglru_backward_out_proj",
    )(xl, z, hs, res, wg[1], bg[1], lam[1:2], h0[1], w_out, gate, final_g.reshape(1, d))
    return out, jnp.stack([h_fwd, h_bwd])


def _final_norm_kernel(x_ref, g_ref, o_ref):
    nb, tm, d = o_ref.shape
    x = x_ref[...]
    y = x * lax.rsqrt(jnp.mean(x * x, axis=-1, keepdims=True) + EPS) * g_ref[...]
    o_ref[...] = jnp.swapaxes(y.reshape(tm, nb, d), 0, 1)


def _final_norm(h_tb, nb, g):
    d = g.shape[0]
    t = h_tb.shape[0] // nb
    tm = min(t, PROJ_STEPS)
    return pl.pallas_call(
        _final_norm_kernel,
        grid=(t // tm,),
        in_specs=[pl.BlockSpec((tm * nb, d), lambda i: (i, 0)), pl.BlockSpec((1, d), lambda i: (0, 0))],
        out_specs=pl.BlockSpec((nb, tm, d), lambda i: (0, i, 0)),
        out_shape=jax.ShapeDtypeStruct((nb, t, d), F32),
        compiler_params=_cparams("parallel"),
        name="final_rmsnorm",
    )(h_tb, g.reshape(1, d))


def kernel(x, c, ctx, c_ctx, norm_g, w_ada, b_ada, w_in_e, conv_a, rpb, w_out_e, w_in_o, conv_c, conv_c_b,
           w_ra, b_ra, w_ri, b_ri, lru_lam, w_out_o, final_g):
    bsz, s, d = x.shape
    depth = norm_g.shape[0]
    w_conv = conv_a.shape[2]
    n_heads = rpb.shape[1]
    win_r, win_c = (rpb.shape[2] + 1) // 2, (rpb.shape[3] + 1) // 2
    w_na = (w_in_e.shape[2] - 4 * w_conv) // 4
    dh = w_na // n_heads
    off_q = 4 * w_conv
    w_rnn = conv_c.shape[2]
    n_blk, blk = w_ra.shape[2], w_ra.shape[3]
    rows = s // GRID_W

    mod = _modulation(c, c_ctx, w_ada, b_ada)

    q_scale = jnp.ones((w_in_e.shape[2],), F32).at[off_q:off_q + w_na].set(dh ** -0.5 * LOG2E)
    w_in_e_bf = (w_in_e * q_scale).astype(BF16)
    w_out_e_bf = w_out_e.astype(BF16)
    w_in_o_bf = w_in_o.astype(BF16)
    w_out_o_bf = w_out_o.astype(BF16)
    n_even, n_odd = w_in_e.shape[0], w_in_o.shape[0]
    bias_vec = _na_bias_vectors(rpb.reshape((n_even * n_heads,) + rpb.shape[2:]), rows, win_r, win_c)
    bias_vec = bias_vec.reshape((n_even, n_heads) + bias_vec.shape[1:])
    wg_all = (0.5 * jnp.concatenate([w_ra, w_ri], axis=-1)).astype(BF16)
    bg_all = 0.5 * jnp.concatenate([b_ra.reshape(n_odd, 2, n_blk, 1, blk),
                                    b_ri.reshape(n_odd, 2, n_blk, 1, blk)], axis=-1)

    h_lat, lat_layout = x, "bsd"
    h_ctx, ctx_layout = ctx, "bsd"
    for layer in range(depth):
        need_ctx = layer < depth - 1
        m_lat = mod[layer, :bsz]
        m_ctx = jnp.broadcast_to(mod[layer, bsz], (bsz, 3 * d))
        shift, scale, gate = (m_lat[:, k * d:(k + 1) * d] for k in range(3))
        shift_c, scale_c, gate_c = (m_ctx[:, k * d:(k + 1) * d] for k in range(3))
        g = norm_g[layer]
        i = layer // 2
        if layer % 2 == 0:
            w_in = w_in_e_bf[i]
            w_out = w_out_e_bf[i]
            z = _in_proj(h_lat, lat_layout, bsz, g, scale, shift, w_in, "bsd")
            zc = _in_proj(h_ctx, ctx_layout, bsz, g, scale_c, shift_c, w_in, "bsd")
            y_a = _conv_mix(z, conv_a[i], w_conv)
            y_b = _na_attn(z, zc, off_q, off_q + w_na, w_na, dh, bias_vec[i], win_r, win_c)
            h_lat = _out_proj(y_a, y_b, w_out[:w_conv], w_out[w_conv:], h_lat, lat_layout, gate)
            lat_layout = "tb"
            if need_ctx:
                yc_a = _conv_mix(zc, conv_a[i], w_conv)
                yc_b = _ctx_attn(zc, off_q, w_na, dh)
                h_ctx = _out_proj(yc_a, yc_b, w_out[:w_conv], w_out[w_conv:], h_ctx, ctx_layout, gate_c)
                ctx_layout = "tb"
        else:
            assert lat_layout == "tb" and ctx_layout == "tb"
            w_in = w_in_o_bf[i]
            z = _in_proj(h_lat, "tb", bsz, g, scale, shift, w_in, "tb")
            zc = _in_proj(h_ctx, "tb", bsz, g, scale_c, shift_c, w_in, "tb")
            wg, bg, w_out = wg_all[i], bg_all[i], w_out_o_bf[i]
            h0 = jnp.zeros((2, bsz, w_rnn), F32)
            new_ctx, h_ctx_t = _rglru(zc, h_ctx, bsz, w_rnn, conv_c[i], conv_c_b[i],
                                      wg, bg, lru_lam[i], h0, w_out, gate_c, final_g, False)
            is_last = layer == depth - 1
            h_lat, _ = _rglru(z, h_lat, bsz, w_rnn, conv_c[i], conv_c_b[i],
                              wg, bg, lru_lam[i], h_ctx_t, w_out, gate, final_g, is_last)
            if is_last:
                return h_lat
            if need_ctx:
                h_ctx = new_ctx
    assert lat_layout == "tb"
    return _final_norm(h_lat, bsz, final_g)
```

```python
import functools

import numpy as np
import jax
import jax.numpy as jnp
from jax import lax
from jax.experimental import pallas as pl
from jax.experimental.pallas import tpu as pltpu

GRID_W = 64
EPS = 1e-6
LRU_C = 8.0
MASK_NEG = -1e30
LOG2E = 1.4426950408889634
ATTN_ROWS = 4
NA_GROUP_UNROLL = 8
LANES = 128
VMEM_LIMIT_BYTES = 48 * 1024 * 1024

F32 = jnp.float32
BF16 = jnp.bfloat16


def _cparams(*sem):
    return pltpu.CompilerParams(dimension_semantics=sem, vmem_limit_bytes=VMEM_LIMIT_BYTES)


def _sigmoid(x):
    return 1.0 / (1.0 + jnp.exp(-x))


def _silu(x):
    return x * _sigmoid(x)


def _mod_kernel(s_ref, w_ref, b_ref, o_ref):
    a = _silu(s_ref[...])
    o_ref[...] = jnp.dot(a, w_ref[...], preferred_element_type=F32,
                         precision=lax.Precision.HIGHEST) + b_ref[...]


def _modulation(c, c_ctx, w_ada, b_ada):
    depth, d, n3 = w_ada.shape
    bsz = c.shape[0]
    rows = -(-(bsz + 1) // 8) * 8
    s_in = jnp.zeros((rows, d), F32).at[:bsz].set(c).at[bsz].set(c_ctx)
    tn = d
    return pl.pallas_call(
        _mod_kernel,
        grid=(depth, n3 // tn),
        in_specs=[pl.BlockSpec((rows, d), lambda l, j: (0, 0)),
                  pl.BlockSpec((None, d, tn), lambda l, j: (l, 0, j)),
                  pl.BlockSpec((None, 1, tn), lambda l, j: (l, 0, j))],
        out_specs=pl.BlockSpec((None, rows, tn), lambda l, j: (l, 0, j)),
        out_shape=jax.ShapeDtypeStruct((depth, rows, n3), F32),
        compiler_params=_cparams("parallel", "parallel"),
        name="adaln_mod",
    )(s_in, w_ada, b_ada.reshape(depth, 1, n3))


PROJ_STEPS = 128
PROJ_COLS = 2048


def _in_proj_kernel(x_ref, g_ref, sc_ref, sh_ref, w_ref, o_ref, u_ref, *, nb, tm, in_tb, out_tb):
    d = x_ref.shape[-1]

    @pl.when(pl.program_id(1) == 0)
    def _():
        x = x_ref[...]
        y = x * lax.rsqrt(jnp.mean(x * x, axis=-1, keepdims=True) + EPS) * g_ref[...]
        sc = 1.0 + sc_ref[...]
        sh = sh_ref[...]
        if in_tb:
            u = y.reshape(tm, nb, d) * sc[None] + sh[None]
        else:
            u = y * sc[:, None, :] + sh[:, None, :]
        if in_tb != out_tb:
            u = jnp.swapaxes(u, 0, 1)
        u_ref[...] = u.reshape(tm * nb, d).astype(u_ref.dtype)

    z = jnp.dot(u_ref[...], w_ref[...], preferred_element_type=F32)
    o_ref[...] = z.reshape(o_ref.shape).astype(o_ref.dtype)


def _in_proj(h, h_layout, nb, g, scale, shift, w, out_layout):
    d, n = w.shape
    t = h.shape[1] if h_layout == "bsd" else h.shape[0] // nb
    tm = min(t, PROJ_STEPS)
    tn = min(n, PROJ_COLS)
    if h_layout == "bsd":
        x_spec = pl.BlockSpec((nb, tm, d), lambda i, j: (0, i, 0))
    else:
        x_spec = pl.BlockSpec((tm * nb, d), lambda i, j: (i, 0))
    if out_layout == "bsd":
        o_spec = pl.BlockSpec((nb, tm, tn), lambda i, j: (0, i, j))
        o_shape = jax.ShapeDtypeStruct((nb, t, n), BF16)
    else:
        o_spec = pl.BlockSpec((tm * nb, tn), lambda i, j: (i, j))
        o_shape = jax.ShapeDtypeStruct((t * nb, n), BF16)
    vec_spec = pl.BlockSpec((nb, d), lambda i, j: (0, 0))
    return pl.pallas_call(
        functools.partial(_in_proj_kernel, nb=nb, tm=tm, in_tb=h_layout == "tb", out_tb=out_layout == "tb"),
        grid=(t // tm, n // tn),
        in_specs=[x_spec,
                  pl.BlockSpec((1, d), lambda i, j: (0, 0)),
                  vec_spec, vec_spec,
                  pl.BlockSpec((d, tn), lambda i, j: (0, j))],
        out_specs=o_spec,
        out_shape=o_shape,
        scratch_shapes=[pltpu.VMEM((tm * nb, d), BF16)],
        compiler_params=_cparams("parallel", "arbitrary"),
        name="norm_in_proj",
    )(h, g.reshape(1, d), scale, shift, w)


def _out_proj_kernel(ya_ref, yb_ref, wa_ref, wb_ref, h_ref, gt_ref, o_ref, *, nb, tm, in_tb):
    d = o_ref.shape[-1]
    ya = ya_ref[...].reshape(nb * tm, ya_ref.shape[-1])
    yb = yb_ref[...].reshape(nb * tm, yb_ref.shape[-1])
    proj = jnp.dot(ya, wa_ref[...], preferred_element_type=F32)
    proj += jnp.dot(yb, wb_ref[...], preferred_element_type=F32)
    gated = proj.reshape(nb, tm, d) * gt_ref[...][:, None, :]
    if in_tb:
        o_ref[...] = h_ref[...] + jnp.swapaxes(gated, 0, 1).reshape(tm * nb, d)
    else:
        o_ref[...] = jnp.swapaxes(h_ref[...] + gated, 0, 1).reshape(tm * nb, d)


def _out_proj(ya, yb, wa, wb, h, h_layout, gate):
    nb, t, wa_in = ya.shape
    wb_in = yb.shape[2]
    d = wa.shape[1]
    tm = min(t, PROJ_STEPS)
    if h_layout == "bsd":
        h_spec = pl.BlockSpec((nb, tm, d), lambda i: (0, i, 0))
    else:
        h_spec = pl.BlockSpec((tm * nb, d), lambda i: (i, 0))
    return pl.pallas_call(
        functools.partial(_out_proj_kernel, nb=nb, tm=tm, in_tb=h_layout == "tb"),
        grid=(t // tm,),
        in_specs=[pl.BlockSpec((nb, tm, wa_in), lambda i: (0, i, 0)),
                  pl.BlockSpec((nb, tm, wb_in), lambda i: (0, i, 0)),
                  pl.BlockSpec((wa_in, d), lambda i: (0, 0)),
                  pl.BlockSpec((wb_in, d), lambda i: (0, 0)),
                  h_spec,
                  pl.BlockSpec((nb, d), lambda i: (0, 0))],
        out_specs=pl.BlockSpec((tm * nb, d), lambda i: (i, 0)),
        out_shape=jax.ShapeDtypeStruct((t * nb, d), F32),
        compiler_params=_cparams("parallel"),
        name="out_proj_residual",
    )(ya, yb, wa, wb, h, gate)


CONV_CHUNK = 256
CONV_PAD = 8


def _conv_mix_kernel(zb_ref, zc_ref, zx_ref, zg_ref, w_ref, o_ref, p_ref, *, t, chunk):
    tc = o_ref.shape[1]
    pad = jnp.zeros((CONV_PAD, tc), F32)
    p_ref[0:CONV_PAD, :] = pad
    p_ref[t + CONV_PAD:t + 2 * CONV_PAD, :] = pad
    for s in range(0, t, chunk):
        p_ref[CONV_PAD + s:CONV_PAD + s + chunk, :] = (
            zc_ref[s:s + chunk, :].astype(F32) * zx_ref[s:s + chunk, :].astype(F32))
    w = w_ref[...]
    n_ext = chunk + 2 * CONV_PAD
    for s in range(0, t, chunk):
        ext = p_ref[s:s + n_ext, :]
        prev = pltpu.roll(ext, 1, axis=0)[CONV_PAD:CONV_PAD + chunk]
        nxt = pltpu.roll(ext, n_ext - 1, axis=0)[CONV_PAD:CONV_PAD + chunk]
        cur = ext[CONV_PAD:CONV_PAD + chunk]
        conv = w[0:1] * prev + w[1:2] * cur + w[2:3] * nxt
        y = zb_ref[s:s + chunk, :].astype(F32) * conv * _silu(zg_ref[s:s + chunk, :].astype(F32))
        o_ref[s:s + chunk, :] = y.astype(o_ref.dtype)


def _conv_mix(z, conv_w, w_conv):
    bsz, t, _ = z.shape
    assert conv_w.shape[0] == 3
    tc = min(w_conv, 256)
    nc = w_conv // tc
    chunk = min(t, CONV_CHUNK)

    def zspec(group):
        return pl.BlockSpec((None, t, tc), lambda b, j, group=group: (b, 0, group * nc + j))

    return pl.pallas_call(
        functools.partial(_conv_mix_kernel, t=t, chunk=chunk),
        grid=(bsz, nc),
        in_specs=[zspec(0), zspec(1), zspec(2), zspec(3),
                  pl.BlockSpec((3, tc), lambda b, j: (0, j))],
        out_specs=pl.BlockSpec((None, t, tc), lambda b, j: (b, 0, j)),
        out_shape=jax.ShapeDtypeStruct((bsz, t, w_conv), BF16),
        scratch_shapes=[pltpu.VMEM((t + 2 * CONV_PAD, tc), F32)],
        compiler_params=_cparams("parallel", "parallel"),
        name="gated_conv3",
    )(z, z, z, z, conv_w)


def _head_masks(dh):
    lane = lax.broadcasted_iota(jnp.int32, (1, LANES), 1)
    return [(lane >= i * dh) & (lane < (i + 1) * dh) for i in range(LANES // dh)]


def _row_reduce(op, reduce_fn, *arrays):
    tiles = [x[:, j:j + LANES] for x in arrays if x.shape[1] % LANES == 0 for j in range(0, x.shape[1], LANES)]
    parts = [x for x in arrays if x.shape[1] % LANES != 0]
    while len(tiles) > 1:
        tiles = [op(a, b) for a, b in zip(tiles[0::2], tiles[1::2])] + (tiles[-1:] if len(tiles) % 2 else [])
    out = None
    for x in tiles + parts:
        r = reduce_fn(x, axis=-1, keepdims=True)
        out = r if out is None else op(out, r)
    return out


def _dot_nt(a, b):
    return lax.dot_general(a, b, (((1,), (1,)), ((), ())), preferred_element_type=F32)


def _ctx_attn_kernel(q_ref, k_ref, v_ref, g_ref, o_ref, *, dh):
    masks = _head_masks(dh)
    for c0 in range(0, q_ref.shape[1], LANES):
        cols = slice(c0, c0 + LANES)
        q = q_ref[:, cols]
        k = k_ref[:, cols]
        v = v_ref[:, cols]
        out = None
        for msk in masks:
            qh = jnp.where(msk, q, jnp.zeros_like(q))
            s = _dot_nt(qh, k)
            p = jnp.exp2(s - jnp.max(s, axis=-1, keepdims=True))
            o = jnp.dot(p.astype(BF16), v, preferred_element_type=F32) / jnp.sum(p, axis=-1, keepdims=True)
            out = o if out is None else jnp.where(msk, o, out)
        o_ref[:, cols] = (out * _silu(g_ref[:, cols].astype(F32))).astype(o_ref.dtype)


def _ctx_attn(zc, off_q, w_na, dh):
    bsz, lc, _ = zc.shape
    assert off_q % w_na == 0 and w_na % LANES == 0

    def spec(off):
        return pl.BlockSpec((None, lc, w_na), lambda b, off=off: (b, 0, off // w_na))

    return pl.pallas_call(
        functools.partial(_ctx_attn_kernel, dh=dh),
        grid=(bsz,),
        in_specs=[spec(off_q), spec(off_q + w_na), spec(off_q + 2 * w_na), spec(off_q + 3 * w_na)],
        out_specs=pl.BlockSpec((None, lc, w_na), lambda b: (b, 0, 0)),
        out_shape=jax.ShapeDtypeStruct((bsz, lc, w_na), BF16),
        compiler_params=_cparams("parallel"),
        name="context_attention",
    )(zc, zc, zc, zc)


def _build_bias_table(vec_ref, bias_ref, win_c):
    n_h, n_v, r_grp, n_pair = vec_ref.shape[:4]
    qcol = lax.broadcasted_iota(jnp.int32, (GRID_W, LANES), 0)
    lane = lax.broadcasted_iota(jnp.int32, (GRID_W, LANES), 1)
    left_half = lane < GRID_W
    kcol = jnp.where(left_half, lane, lane - GRID_W)
    first = jnp.clip(qcol - win_c // 2, 0, GRID_W - win_c)
    col_ok = (kcol >= first) & (kcol < first + win_c)
    for h in range(n_h):
        for v in range(n_v):
            for i in range(r_grp):
                for m in range(n_pair):
                    skew = [pltpu.roll(jnp.broadcast_to(vec_ref[h, v, i, m, s:s + 1, :], (GRID_W, LANES)),
                                       0, 1, stride=1, stride_axis=0) for s in range(2)]
                    tile = jnp.where(col_ok, jnp.where(left_half, skew[0], skew[1]), MASK_NEG)
                    bias_ref[h, v, i * GRID_W:(i + 1) * GRID_W, m * LANES:(m + 1) * LANES] = tile


def _na_attn_kernel(q_ref, k_ref, v_ref, g_ref, kc_ref, vc_ref, vec_ref, o_ref, bias_ref, *,
                    dh, n_groups, rows, win_r, win_c):
    masks = _head_masks(dh)
    rq = ATTN_ROWS * GRID_W
    wrows = ATTN_ROWS + win_r
    rk = wrows * GRID_W
    kc = kc_ref[...]
    vc = vc_ref[...]

    @pl.when(pl.program_id(1) == 0)
    def _():
        _build_bias_table(vec_ref, bias_ref, win_c)

    def group(g, carry):
        q0 = pl.multiple_of(g * rq, rq)
        ws = jnp.clip(g * ATTN_ROWS - win_r // 2, 0, rows - wrows)
        k0 = pl.multiple_of(ws * GRID_W, GRID_W)
        var = jnp.where(g == 0, 0, jnp.where(g == n_groups - 1, 2, 1))
        q = q_ref[pl.ds(q0, rq), :]
        kw = k_ref[pl.ds(k0, rk), :]
        vw = v_ref[pl.ds(k0, rk), :]
        out = None
        for hi, msk in enumerate(masks):
            qh = jnp.where(msk, q, jnp.zeros_like(q))
            s_w = _dot_nt(qh, kw) + bias_ref[hi, var]
            s_c = _dot_nt(qh, kc)
            m = _row_reduce(jnp.maximum, jnp.max, s_w, s_c)
            p_w = jnp.exp2(s_w - m)
            p_c = jnp.exp2(s_c - m)
            l = _row_reduce(jnp.add, jnp.sum, p_w, p_c)
            o = (jnp.dot(p_w.astype(BF16), vw, preferred_element_type=F32)
                 + jnp.dot(p_c.astype(BF16), vc, preferred_element_type=F32)) / l
            out = o if out is None else jnp.where(msk, o, out)
        gt = g_ref[pl.ds(q0, rq), :].astype(F32)
        o_ref[pl.ds(q0, rq), :] = (out * _silu(gt)).astype(o_ref.dtype)
        return carry

    lax.fori_loop(0, n_groups, group, 0, unroll=min(n_groups, NA_GROUP_UNROLL))


def _na_bias_vectors(rpb, rows, win_r, win_c):
    assert 2 * GRID_W == LANES
    n_heads = rpb.shape[0]
    r_grp = ATTN_ROWS
    wrows = r_grp + win_r
    assert wrows % 2 == 0
    n_dr = 2 * win_r - 1
    row_idx = np.full((3, r_grp, wrows), n_dr, np.int32)
    for v, r0 in enumerate((0, r_grp, rows - r_grp)):
        ws = int(np.clip(r0 - win_r // 2, 0, rows - wrows))
        for i in range(r_grp):
            r = r0 + i
            rs = int(np.clip(r - win_r // 2, 0, rows - win_r))
            for j in range(wrows):
                rp = ws + j
                if rs <= rp < rs + win_r:
                    row_idx[v, i, j] = rp - r + win_r - 1
    pad = GRID_W - win_c
    padded = jnp.pad(rpb.astype(F32) * LOG2E, ((0, 0), (0, 1), (pad, pad + 1)), constant_values=MASK_NEG)
    band = jnp.take(padded, row_idx.reshape(-1), axis=1).reshape(n_heads, 3, r_grp, wrows // 2, 2, LANES)
    even = jnp.roll(band[..., 0, :], -(GRID_W - 1), axis=-1)
    odd = jnp.roll(band[..., 1, :], 1, axis=-1)
    return jnp.stack([even, odd], axis=-2)


def _na_attn(z, zc, off_q, off_kc, w_na, dh, bias_vec, win_r, win_c):
    bsz, s, _ = z.shape
    lc = zc.shape[1]
    npair = w_na // LANES
    rows = s // GRID_W
    wrows = ATTN_ROWS + win_r
    assert rows % ATTN_ROWS == 0 and rows >= wrows and ATTN_ROWS >= win_r // 2
    assert LANES % dh == 0
    hp = LANES // dh
    n_groups = rows // ATTN_ROWS

    def zspec(off):
        return pl.BlockSpec((None, s, LANES), lambda p, b, off=off: (b, 0, off // LANES + p))

    def cspec(off):
        return pl.BlockSpec((None, lc, LANES), lambda p, b, off=off: (b, 0, off // LANES + p))

    return pl.pallas_call(
        functools.partial(_na_attn_kernel, dh=dh, n_groups=n_groups, rows=rows, win_r=win_r, win_c=win_c),
        grid=(npair, bsz),
        in_specs=[zspec(off_q), zspec(off_q + w_na), zspec(off_q + 2 * w_na), zspec(off_q + 3 * w_na),
                  cspec(off_kc), cspec(off_kc + w_na),
                  pl.BlockSpec((hp,) + bias_vec.shape[1:], lambda p, b: (p, 0, 0, 0, 0, 0))],
        out_specs=pl.BlockSpec((None, s, LANES), lambda p, b: (b, 0, p)),
        out_shape=jax.ShapeDtypeStruct((bsz, s, w_na), BF16),
        scratch_shapes=[pltpu.VMEM((hp, 3, ATTN_ROWS * GRID_W, wrows * GRID_W), F32)],
        compiler_params=_cparams("arbitrary", "arbitrary"),
        name="neighbourhood_attention",
    )(z, z, z, z, zc, zc, bias_vec)


LRU_STEPS = 64
LRU_BLOCK_UNROLL = 4
LRU_SCAN_UNROLL = 4


def _lru_conv_block(zx_ref, zp_ref, zn_ref, cw_ref, cb_ref, cols, has_prev, has_next, nb):
    n = zx_ref.shape[0]
    k_taps = cw_ref.shape[0]
    left = (k_taps - 1) // 2
    x = zx_ref[:, cols].astype(F32)
    xp = jnp.where(has_prev, zp_ref[:, cols].astype(F32), 0.0)[2 * nb - left * nb:]
    xn = jnp.where(has_next, zn_ref[:, cols].astype(F32), 0.0)
    xe = jnp.concatenate([xp, x, xn], axis=0)
    cw = cw_ref[:, cols]
    xl = cb_ref[:, cols]
    for k in range(k_taps):
        xl = xl + cw[k:k + 1] * xe[k * nb:k * nb + n]
    return xl


def _lru_coeff_block(xl, j, cols, wg_ref, bg_ref, lam_ref, a_ref, b_ref, blk):
    tg = jnp.tanh(jnp.dot(xl.astype(BF16), wg_ref[j], preferred_element_type=F32) + bg_ref[j])
    half_c = (0.5 * LRU_C) * jax.nn.log_sigmoid(lam_ref[:, cols])
    log_a = half_c + half_c * tg[:, :blk]
    ig2 = 1.0 + tg[:, blk:]
    a_ref[:, cols] = jnp.exp(log_a)
    th = jnp.tanh(log_a)
    num = -0.5 * th
    half_root = jnp.where(num > 0.0, num * lax.rsqrt(num * (1.0 - th)), 0.0)
    b_ref[:, cols] = half_root * ig2 * xl


def _lru_fwd_kernel(zx_ref, zp_ref, zn_ref, cw_ref, cb_ref, wg_ref, bg_ref, lam_ref, h0_ref,
                    hs_ref, xl_ref, hT_ref, a_ref, b_ref, h_ref, *, nb, n_blk, blk, steps):
    i = pl.program_id(0)
    last = pl.num_programs(0) - 1

    @pl.when(i == 0)
    def _():
        h_ref[...] = h0_ref[...]

    def block(j, carry):
        cols = pl.ds(pl.multiple_of(j * blk, blk), blk)
        xl = _lru_conv_block(zx_ref, zp_ref, zn_ref, cw_ref, cb_ref, cols, i > 0, i < last, nb)
        xl_ref[:, cols] = xl
        _lru_coeff_block(xl, j, cols, wg_ref, bg_ref, lam_ref, a_ref, b_ref, blk)
        return carry

    lax.fori_loop(0, n_blk, block, 0, unroll=min(n_blk, LRU_BLOCK_UNROLL))

    def step(t, h):
        rows = pl.ds(pl.multiple_of(t * nb, nb), nb)
        h = a_ref[rows, :] * h + b_ref[rows, :]
        hs_ref[rows, :] = h
        return h

    h_ref[...] = lax.fori_loop(0, steps, step, h_ref[...], unroll=min(steps, LRU_SCAN_UNROLL))

    @pl.when(i == last)
    def _():
        hT_ref[...] = h_ref[...]


def _lru_bwd_kernel(xl_ref, zg_ref, hs_ref, res_ref, wg_ref, bg_ref, lam_ref,
                    h0_ref, wo_ref, gm_ref, fg_ref, o_ref, hT_ref, a_ref, b_ref, h_ref, *,
                    nb, n_blk, blk, steps, final_norm):
    i = pl.program_id(0)
    last = pl.num_programs(0) - 1

    @pl.when(i == 0)
    def _():
        h_ref[...] = h0_ref[...]

    def block(j, carry):
        cols = pl.ds(pl.multiple_of(j * blk, blk), blk)
        _lru_coeff_block(xl_ref[:, cols], j, cols, wg_ref, bg_ref, lam_ref, a_ref, b_ref, blk)
        return carry

    lax.fori_loop(0, n_blk, block, 0, unroll=min(n_blk, LRU_BLOCK_UNROLL))

    def step(k, h):
        rows = pl.ds(pl.multiple_of((steps - 1 - k) * nb, nb), nb)
        h = a_ref[rows, :] * h + b_ref[rows, :]
        b_ref[rows, :] = hs_ref[rows, :] + h
        return h

    h_ref[...] = lax.fori_loop(0, steps, step, h_ref[...], unroll=min(steps, LRU_SCAN_UNROLL))

    @pl.when(i == last)
    def _():
        hT_ref[...] = h_ref[...]

    y = (b_ref[...] * _silu(zg_ref[...].astype(F32))).astype(BF16)
    proj = jnp.dot(y, wo_ref[...], preferred_element_type=F32)
    d = proj.shape[1]
    gated = (proj.reshape(steps, nb, d) * gm_ref[...][None]).reshape(steps * nb, d)
    out = res_ref[...] + gated
    if final_norm:
        out = out * lax.rsqrt(jnp.mean(out * out, axis=-1, keepdims=True) + EPS) * fg_ref[...]
        o_ref[...] = jnp.swapaxes(out.reshape(steps, nb, d), 0, 1)
    else:
        o_ref[...] = out


def _rglru(z, res, nb, w_rnn, conv_w, conv_b, wg, bg, lam, h0, w_out, gate, final_g, final_norm):
    n_rows = z.shape[0]
    d = res.shape[1]
    t = n_rows // nb
    steps = min(t, LRU_STEPS)
    n_tiles = t // steps
    tile = steps * nb
    halo = 2 * nb
    n_halo = n_rows // halo
    per_tile = tile // halo
    n_blk, blk = wg.shape[1], wg.shape[2]
    assert nb % 8 == 0 and conv_w.shape[0] == 4

    kw = dict(nb=nb, n_blk=n_blk, blk=blk, steps=steps)
    const2 = lambda i: (0, 0)
    const3 = lambda i: (0, 0, 0)

    def specs(tile_of):
        return [pl.BlockSpec((tile, w_rnn), lambda i: (tile_of(i), 0)),
                pl.BlockSpec((halo, w_rnn), lambda i: (jnp.maximum(tile_of(i) * per_tile - 1, 0), 0)),
                pl.BlockSpec((halo, w_rnn), lambda i: (jnp.minimum((tile_of(i) + 1) * per_tile, n_halo - 1), 0))]

    conv_specs = [pl.BlockSpec(conv_w.shape, const2), pl.BlockSpec((1, w_rnn), const2)]
    gate_specs = [pl.BlockSpec((n_blk, blk, 2 * blk), const3), pl.BlockSpec((n_blk, 1, 2 * blk), const3),
                  pl.BlockSpec((1, w_rnn), const2), pl.BlockSpec((nb, w_rnn), const2)]
    scratch = [pltpu.VMEM((tile, w_rnn), F32), pltpu.VMEM((tile, w_rnn), F32), pltpu.VMEM((nb, w_rnn), F32)]

    fwd_tile = lambda i: i
    hs, xl, h_fwd = pl.pallas_call(
        functools.partial(_lru_fwd_kernel, **kw),
        grid=(n_tiles,),
        in_specs=specs(fwd_tile) + conv_specs + gate_specs,
        out_specs=[pl.BlockSpec((tile, w_rnn), lambda i: (i, 0)), pl.BlockSpec((tile, w_rnn), lambda i: (i, 0)),
                   pl.BlockSpec((nb, w_rnn), const2)],
        out_shape=[jax.ShapeDtypeStruct((n_rows, w_rnn), F32), jax.ShapeDtypeStruct((n_rows, w_rnn), F32),
                   jax.ShapeDtypeStruct((nb, w_rnn), F32)],
        scratch_shapes=scratch,
        compiler_params=_cparams("arbitrary"),
        name="rglru_forward",
    )(z, z, z, conv_w, conv_b.reshape(1, w_rnn), wg[0], bg[0], lam[0:1], h0[0])

    bwd_tile = lambda i: n_tiles - 1 - i
    if final_norm:
        out_spec = pl.BlockSpec((nb, steps, d), lambda i: (0, bwd_tile(i), 0))
        out_shape = jax.ShapeDtypeStruct((nb, t, d), F32)
    else:
        out_spec = pl.BlockSpec((tile, d), lambda i: (bwd_tile(i), 0))
        out_shape = jax.ShapeDtypeStruct((n_rows, d), F32)
    out, h_bwd = pl.pallas_call(
        functools.partial(_lru_bwd_kernel, final_norm=final_norm, **kw),
        grid=(n_tiles,),
        in_specs=([pl.BlockSpec((tile, w_rnn), lambda i: (bwd_tile(i), 0)),
                   pl.BlockSpec((tile, w_rnn), lambda i: (bwd_tile(i), 1)),
                   pl.BlockSpec((tile, w_rnn), lambda i: (bwd_tile(i), 0)),
                   pl.BlockSpec((tile, d), lambda i: (bwd_tile(i), 0))]
                  + gate_specs
                  + [pl.BlockSpec((w_rnn, d), const2), pl.BlockSpec((nb, d), const2), pl.BlockSpec((1, d), const2)]),
        out_specs=[out_spec, pl.BlockSpec((nb, w_rnn), const2)],
        out_shape=[out_shape, jax.ShapeDtypeStruct((nb, w_rnn), F32)],
        scratch_shapes=scratch,
        compiler_params=_cparams("arbitrary"),
        name="rglru_backward_out_proj",
    )(xl, z, hs, res, wg[1], bg[1], lam[1:2], h0[1], w_out, gate, final_g.reshape(1, d))
    return out, jnp.stack([h_fwd, h_bwd])


def _final_norm_kernel(x_ref, g_ref, o_ref):
    nb, tm, d = o_ref.shape
    x = x_ref[...]
    y = x * lax.rsqrt(jnp.mean(x * x, axis=-1, keepdims=True) + EPS) * g_ref[...]
    o_ref[...] = jnp.swapaxes(y.reshape(tm, nb, d), 0, 1)


def _final_norm(h_tb, nb, g):
    d = g.shape[0]
    t = h_tb.shape[0] // nb
    tm = min(t, PROJ_STEPS)
    return pl.pallas_call(
        _final_norm_kernel,
        grid=(t // tm,),
        in_specs=[pl.BlockSpec((tm * nb, d), lambda i: (i, 0)), pl.BlockSpec((1, d), lambda i: (0, 0))],
        out_specs=pl.BlockSpec((nb, tm, d), lambda i: (0, i, 0)),
        out_shape=jax.ShapeDtypeStruct((nb, t, d), F32),
        compiler_params=_cparams("parallel"),
        name="final_rmsnorm",
    )(h_tb, g.reshape(1, d))


def kernel(x, c, ctx, c_ctx, norm_g, w_ada, b_ada, w_in_e, conv_a, rpb, w_out_e, w_in_o, conv_c, conv_c_b,
           w_ra, b_ra, w_ri, b_ri, lru_lam, w_out_o, final_g):
    bsz, s, d = x.shape
    depth = norm_g.shape[0]
    w_conv = conv_a.shape[2]
    n_heads = rpb.shape[1]
    win_r, win_c = (rpb.shape[2] + 1) // 2, (rpb.shape[3] + 1) // 2
    w_na = (w_in_e.shape[2] - 4 * w_conv) // 4
    dh = w_na // n_heads
    off_q = 4 * w_conv
    w_rnn = conv_c.shape[2]
    n_blk, blk = w_ra.shape[2], w_ra.shape[3]
    rows = s // GRID_W

    mod = _modulation(c, c_ctx, w_ada, b_ada)

    q_scale = jnp.ones((w_in_e.shape[2],), F32).at[off_q:off_q + w_na].set(dh ** -0.5 * LOG2E)
    w_in_e_bf = (w_in_e * q_scale).astype(BF16)
    w_out_e_bf = w_out_e.astype(BF16)
    w_in_o_bf = w_in_o.astype(BF16)
    w_out_o_bf = w_out_o.astype(BF16)
    n_even, n_odd = w_in_e.shape[0], w_in_o.shape[0]
    bias_vec = _na_bias_vectors(rpb.reshape((n_even * n_heads,) + rpb.shape[2:]), rows, win_r, win_c)
    bias_vec = bias_vec.reshape((n_even, n_heads) + bias_vec.shape[1:])
    wg_all = (0.5 * jnp.concatenate([w_ra, w_ri], axis=-1)).astype(BF16)
    bg_all = 0.5 * jnp.concatenate([b_ra.reshape(n_odd, 2, n_blk, 1, blk),
                                    b_ri.reshape(n_odd, 2, n_blk, 1, blk)], axis=-1)

    h_lat, lat_layout = x, "bsd"
    h_ctx, ctx_layout = ctx, "bsd"
    for layer in range(depth):
        need_ctx = layer < depth - 1
        m_lat = mod[layer, :bsz]
        m_ctx = jnp.broadcast_to(mod[layer, bsz], (bsz, 3 * d))
        shift, scale, gate = (m_lat[:, k * d:(k + 1) * d] for k in range(3))
        shift_c, scale_c, gate_c = (m_ctx[:, k * d:(k + 1) * d] for k in range(3))
        g = norm_g[layer]
        i = layer // 2
        if layer % 2 == 0:
            w_in = w_in_e_bf[i]
            w_out = w_out_e_bf[i]
            z = _in_proj(h_lat, lat_layout, bsz, g, scale, shift, w_in, "bsd")
            zc = _in_proj(h_ctx, ctx_layout, bsz, g, scale_c, shift_c, w_in, "bsd")
            y_a = _conv_mix(z, conv_a[i], w_conv)
            y_b = _na_attn(z, zc, off_q, off_q + w_na, w_na, dh, bias_vec[i], win_r, win_c)
            h_lat = _out_proj(y_a, y_b, w_out[:w_conv], w_out[w_conv:], h_lat, lat_layout, gate)
            lat_layout = "tb"
            if need_ctx:
                yc_a = _conv_mix(zc, conv_a[i], w_conv)
                yc_b = _ctx_attn(zc, off_q, w_na, dh)
                h_ctx = _out_proj(yc_a, yc_b, w_out[:w_conv], w_out[w_conv:], h_ctx, ctx_layout, gate_c)
                ctx_layout = "tb"
        else:
            assert lat_layout == "tb" and ctx_layout == "tb"
            w_in = w_in_o_bf[i]
            z = _in_proj(h_lat, "tb", bsz, g, scale, shift, w_in, "tb")
            zc = _in_proj(h_ctx, "tb", bsz, g, scale_c, shift_c, w_in, "tb")
            wg, bg, w_out = wg_all[i], bg_all[i], w_out_o_bf[i]
            h0 = jnp.zeros((2, bsz, w_rnn), F32)
            new_ctx, h_ctx_t = _rglru(zc, h_ctx, bsz, w_rnn, conv_c[i], conv_c_b[i],
                                      wg, bg, lru_lam[i], h0, w_out, gate_c, final_g, False)
            is_last = layer == depth - 1
            h_lat, _ = _rglru(z, h_lat, bsz, w_rnn, conv_c[i], conv_c_b[i],
                              wg, bg, lru_lam[i], h_ctx_t, w_out, gate, final_g, is_last)
            if is_last:
                return h_lat
            if need_ctx:
                h_ctx = new_ctx
    assert lat_layout == "tb"
    return _final_norm(h_lat, bsz, final_g)
```
